```python
import jax, jax.numpy as jnp
from jax import lax
import numpy as np

D_MODEL = 2048
BATCH = 8
SEQ = 2048
DEPTH = 2

EPS = 1e-6
D_FF = 4 * D_MODEL
GLA_HEADS = 4
GLA_DK = 128
GLA_DV = 256
GLA_GATE_RANK = 16
GLA_GATE_TAU = 16.0
GLA_CHUNK = 64
MOBA_HEADS = 8
MOBA_HD = 128
MOBA_BLOCK = 256
MOBA_TOPK = 3
MOBA_Q_CHUNK = 16
FOX_HEADS = 8
FOX_HD = 128
FOX_Q_BLOCK = 128
HGRN_HEADS = 8
HGRN_DK = 128
HGRN_DV = 128
HGRN_CHUNK = 64
EVEN_SIZES = (GLA_HEADS * GLA_DK, GLA_HEADS * GLA_DK, GLA_HEADS * GLA_DV, GLA_HEADS * GLA_DV, GLA_GATE_RANK,
              MOBA_HEADS * MOBA_HD, MOBA_HEADS * MOBA_HD, MOBA_HEADS * MOBA_HD)
EVEN_IN = 4 * GLA_HEADS * GLA_DK // 2 + 2 * GLA_HEADS * GLA_DV + GLA_GATE_RANK + 3 * MOBA_HEADS * MOBA_HD
EVEN_MIX = GLA_HEADS * GLA_DV + MOBA_HEADS * MOBA_HD
ODD_SIZES = (FOX_HEADS * FOX_HD, FOX_HEADS * FOX_HD, FOX_HEADS * FOX_HD, FOX_HEADS,
             HGRN_HEADS * HGRN_DK, HGRN_HEADS * HGRN_DK, HGRN_HEADS * HGRN_DV, HGRN_HEADS * HGRN_DV)
ODD_IN = 3 * FOX_HEADS * FOX_HD + FOX_HEADS + 2 * HGRN_HEADS * HGRN_DK + 2 * HGRN_HEADS * HGRN_DV
ODD_MIX = FOX_HEADS * FOX_HD + HGRN_HEADS * HGRN_DV
N_EVEN = (DEPTH + 1) // 2
N_ODD = DEPTH // 2

kernel_name = "hybrid_gla_moba_fox_hgrn2_trunk"

F32 = jnp.float32


def rms_norm(x, g):
    xf = x.astype(F32)
    y = xf * lax.rsqrt(jnp.mean(xf * xf, axis=-1, keepdims=True) + EPS)
    return (y * g.astype(F32)).astype(x.dtype)


def _split(u, sizes):
    out, o = [], 0
    for s in sizes:
        out.append(u[..., o:o + s])
        o += s
    return out


def _heads(t, n):
    B, S, _ = t.shape
    return t.reshape(B, S, n, -1).transpose(0, 2, 1, 3)


def _merge(t):
    B, H, S, d = t.shape
    return t.transpose(0, 2, 1, 3).reshape(B, S, H * d)


def alibi_slopes(n):
    return jnp.exp2(-8.0 * jnp.arange(1, n + 1, dtype=F32) / n)


def chunked_gated_linear_attention(q, k, v, log_g, chunk):
    B, H, S, dk = q.shape
    dv = v.shape[-1]
    n = S // chunk

    def to_chunks(t):
        return jnp.moveaxis(t.reshape(B, H, n, chunk, t.shape[-1]), 2, 0)

    qc, kc, vc = to_chunks(q), to_chunks(k), to_chunks(v)
    bc = jnp.cumsum(to_chunks(log_g.astype(F32)), axis=3)
    causal = jnp.tril(jnp.ones((chunk, chunk), dtype=bool))

    def step(state, inp):
        q_, k_, v_, b_ = inp
        b_last = b_[:, :, -1:, :]
        o_inter = jnp.einsum('bhtd,bhde->bhte', q_ * jnp.exp(b_), state)
        diff = b_[:, :, :, None, :] - b_[:, :, None, :, :]
        decay = jnp.exp(jnp.where(causal[:, :, None], diff, -jnp.inf))
        scores = jnp.einsum('bhtd,bhsd,bhtsd->bhts', q_, k_, decay)
        o = o_inter + jnp.einsum('bhts,bhse->bhte', scores, v_)
        state = jnp.exp(b_last[:, :, 0, :, None]) * state + jnp.einsum(
            'bhsd,bhse->bhde', k_ * jnp.exp(b_last - b_), v_)
        return state, o

    init = jnp.zeros((B, H, dk, dv), F32)
    _, o = lax.scan(step, init, (qc, kc, vc, bc))
    return jnp.moveaxis(o, 0, 2).reshape(B, H, S, dv)


def gla_mixer(q, k, v, g, gate_lr, w2, b2, out_gain):
    qh = _heads(q, GLA_HEADS) * (GLA_DK ** -0.5)
    kh = _heads(k, GLA_HEADS)
    vh = _heads(v, GLA_HEADS)
    log_a = jax.nn.log_sigmoid((gate_lr @ w2 + b2).astype(F32)) / GLA_GATE_TAU
    o = chunked_gated_linear_attention(qh, kh, vh, _heads(log_a, GLA_HEADS), GLA_CHUNK)
    o = rms_norm(o, out_gain)
    return (_merge(o) * jax.nn.silu(g.astype(F32))).astype(g.dtype)


def moba_mixer(q, k, v):
    qh = _heads(q, MOBA_HEADS) * (MOBA_HD ** -0.5)
    kh = _heads(k, MOBA_HEADS)
    vh = _heads(v, MOBA_HEADS)
    B, H, S, d = qh.shape
    BLK, CQ = MOBA_BLOCK, MOBA_Q_CHUNK
    nb = -(-S // BLK)
    sp = nb * BLK
    pad = ((0, 0), (0, 0), (0, sp - S), (0, 0))
    qp, kp, vp = jnp.pad(qh, pad), jnp.pad(kh, pad), jnp.pad(vh, pad)
    kb = kp.reshape(B, H, nb, BLK, d)
    vb = vp.reshape(B, H, nb, BLK, d)
    k_mean = jnp.mean(kb.astype(F32), axis=3)
    qblk = jnp.arange(sp) // BLK
    gate = jnp.einsum('bhtd,bhnd->bhtn', qp.astype(F32), k_mean)
    past = jnp.arange(nb)[None, :] < qblk[:, None]
    gate = jnp.where(past, gate, -jnp.inf)
    topk = min(MOBA_TOPK, nb)
    _, sel = lax.top_k(gate, topk)
    slopes = alibi_slopes(H)
    sl5 = slopes.reshape(1, H, 1, 1, 1)
    sl4 = slopes.reshape(1, H, 1, 1)
    b_ix = jnp.arange(B)[:, None, None, None]
    h_ix = jnp.arange(H)[None, :, None, None]
    offs = jnp.arange(BLK)

    def chunk_attend(c):
        t0 = c * CQ
        t = t0 + jnp.arange(CQ)
        j = t0 // BLK
        q_c = lax.dynamic_slice_in_dim(qp, t0, CQ, axis=2)
        idx = lax.dynamic_slice_in_dim(sel, t0, CQ, axis=2)
        k_sel = kb[b_ix, h_ix, idx]
        v_sel = vb[b_ix, h_ix, idx]
        s_sel = idx[..., None] * BLK + offs
        logit_sel = (jnp.einsum('bhqd,bhqjkd->bhqjk', q_c, k_sel).astype(F32)
                     - sl5 * (t[:, None, None] - s_sel).astype(F32))
        logit_sel = jnp.where(idx[..., None] < j, logit_sel, -jnp.inf)
        k_own = lax.dynamic_slice_in_dim(kp, j * BLK, BLK, axis=2)
        v_own = lax.dynamic_slice_in_dim(vp, j * BLK, BLK, axis=2)
        s_own = j * BLK + offs
        logit_own = (jnp.einsum('bhqd,bhkd->bhqk', q_c, k_own).astype(F32)
                     - sl4 * (t[:, None] - s_own[None, :]).astype(F32))
        logit_own = jnp.where(s_own[None, :] <= t[:, None], logit_own, -jnp.inf)
        logits = jnp.concatenate([logit_sel.reshape(B, H, CQ, topk * BLK), logit_own], axis=-1)
        p = jax.nn.softmax(logits, axis=-1)
        p_sel = p[..., :topk * BLK].reshape(B, H, CQ, topk, BLK)
        p_own = p[..., topk * BLK:]
        return (jnp.einsum('bhqjk,bhqjke->bhqe', p_sel, v_sel)
                + jnp.einsum('bhqk,bhke->bhqe', p_own, v_own))

    o = lax.map(chunk_attend, jnp.arange(sp // CQ))
    o = jnp.moveaxis(o, 0, 2).reshape(B, H, sp, d)[:, :, :S]
    return o


def fox_mixer(q, k, v, f_logit, f_bias):
    qh = _heads(q, FOX_HEADS) * (FOX_HD ** -0.5)
    kh = _heads(k, FOX_HEADS)
    vh = _heads(v, FOX_HEADS)
    B, H, S, d = qh.shape
    log_f = jax.nn.log_sigmoid((f_logit + f_bias).astype(F32))
    c = jnp.cumsum(log_f, axis=1).transpose(0, 2, 1)
    outs = []
    for i in range(S // FOX_Q_BLOCK):
        lo, hi = i * FOX_Q_BLOCK, (i + 1) * FOX_Q_BLOCK
        logits = (jnp.einsum('bhqd,bhkd->bhqk', qh[:, :, lo:hi], kh[:, :, :hi]).astype(F32)
                  + c[:, :, lo:hi, None] - c[:, :, None, :hi])
        causal = jnp.arange(lo, hi)[:, None] >= jnp.arange(hi)[None, :]
        p = jax.nn.softmax(jnp.where(causal, logits, -jnp.inf), axis=-1)
        outs.append(jnp.einsum('bhqk,bhkd->bhqd', p, vh[:, :, :hi]))
    return jnp.concatenate(outs, axis=2)


def hgrn2_mixer(q, f_logit, i, g, lower_bound, out_gain):
    f = lower_bound + (1.0 - lower_bound) * jax.nn.sigmoid(f_logit.astype(F32))
    log_f = jnp.log(f)
    k = 1.0 - f
    qa = jax.nn.silu(q)
    o = chunked_gated_linear_attention(_heads(qa, HGRN_HEADS), _heads(k, HGRN_HEADS),
                                       _heads(i, HGRN_HEADS), _heads(log_f, HGRN_HEADS), HGRN_CHUNK)
    o = rms_norm(o, out_gain)
    return (_merge(o) * jax.nn.silu(g.astype(F32))).astype(g.dtype)


def sqrelu_mlp(x, w_up, w_down):
    return jnp.square(jax.nn.relu(x @ w_up)) @ w_down


def setup_inputs(seed: int = 0) -> dict:
    key = jax.random.key(seed)
    ks = jax.random.split(key, 24)

    def nrm(k, shape, scale):
        return jax.random.normal(k, shape, F32) * scale

    def gain(k, shape):
        return 1.0 + 0.02 * jax.random.normal(k, shape, F32)

    return {
        "x": nrm(ks[0], (BATCH, SEQ, D_MODEL), 1.0),
        "ev_norm_mix": gain(ks[1], (N_EVEN, D_MODEL)),
        "ev_w_in": nrm(ks[2], (N_EVEN, D_MODEL, EVEN_IN), D_MODEL ** -0.5),
        "ev_gla_gate_w2": nrm(ks[3], (N_EVEN, GLA_GATE_RANK, GLA_HEADS * GLA_DK), GLA_GATE_RANK ** -0.5),
        "ev_gla_gate_b": nrm(ks[4], (N_EVEN, GLA_HEADS * GLA_DK), 0.1),
        "ev_gla_out_norm": gain(ks[5], (N_EVEN, GLA_DV)),
        "ev_w_out": nrm(ks[6], (N_EVEN, EVEN_MIX, D_MODEL), EVEN_MIX ** -0.5),
        "ev_norm_mlp": gain(ks[7], (N_EVEN, D_MODEL)),
        "ev_w_up": nrm(ks[8], (N_EVEN, D_MODEL, D_FF), D_MODEL ** -0.5),
        "ev_w_down": nrm(ks[9], (N_EVEN, D_FF, D_MODEL), D_FF ** -0.5),
        "od_norm_mix": gain(ks[10], (N_ODD, D_MODEL)),
        "od_w_in": nrm(ks[11], (N_ODD, D_MODEL, ODD_IN), D_MODEL ** -0.5),
        "od_fox_fgate_b": nrm(ks[12], (N_ODD, FOX_HEADS), 0.1),
        "od_hgrn_out_norm": gain(ks[13], (N_ODD, HGRN_DV)),
        "od_w_out": nrm(ks[14], (N_ODD, ODD_MIX, D_MODEL), ODD_MIX ** -0.5),
        "od_norm_mlp": gain(ks[15], (N_ODD, D_MODEL)),
        "od_w_up": nrm(ks[16], (N_ODD, D_MODEL, D_FF), D_MODEL ** -0.5),
        "od_w_down": nrm(ks[17], (N_ODD, D_FF, D_MODEL), D_FF ** -0.5),
        "hgrn_lb_raw": nrm(ks[18], (DEPTH, HGRN_HEADS * HGRN_DK), 0.1),
        "final_norm": gain(ks[19], (D_MODEL,)),
    }


def reference(x, ev_norm_mix, ev_w_in, ev_gla_gate_w2, ev_gla_gate_b, ev_gla_out_norm, ev_w_out,
              ev_norm_mlp, ev_w_up, ev_w_down, od_norm_mix, od_w_in, od_fox_fgate_b, od_hgrn_out_norm,
              od_w_out, od_norm_mlp, od_w_up, od_w_down, hgrn_lb_raw, final_norm):
    lb_soft = jax.nn.softmax(hgrn_lb_raw.astype(F32), axis=0)
    lb_all = jnp.cumsum(lb_soft, axis=0) - lb_soft[0]

    h = x
    for layer in range(DEPTH):
        e = layer // 2
        if layer % 2 == 0:
            u = rms_norm(h, ev_norm_mix[e]) @ ev_w_in[e]
            gq, gk, gv, gg, glr, mq, mk, mv = _split(u, EVEN_SIZES)
            a = gla_mixer(gq, gk, gv, gg, glr, ev_gla_gate_w2[e], ev_gla_gate_b[e], ev_gla_out_norm[e])
            b = _merge(moba_mixer(mq, mk, mv)).astype(a.dtype)
            mix = jnp.concatenate([a, b], axis=-1) @ ev_w_out[e]
            h = h + mix.astype(h.dtype)
            h = h + sqrelu_mlp(rms_norm(h, ev_norm_mlp[e]), ev_w_up[e], ev_w_down[e]).astype(h.dtype)
        else:
            u = rms_norm(h, od_norm_mix[e]) @ od_w_in[e]
            fq, fk, fv, ff, hq, hf, hi, hg = _split(u, ODD_SIZES)
            c = _merge(fox_mixer(fq, fk, fv, ff, od_fox_fgate_b[e]))
            d = hgrn2_mixer(hq, hf, hi, hg, lb_all[layer], od_hgrn_out_norm[e])
            mix = jnp.concatenate([c.astype(d.dtype), d], axis=-1) @ od_w_out[e]
            h = h + mix.astype(h.dtype)
            h = h + sqrelu_mlp(rms_norm(h, od_norm_mlp[e]), od_w_up[e], od_w_down[e]).astype(h.dtype)
    return rms_norm(h, final_norm)
```

```python
import functools

import jax
import jax.numpy as jnp
from jax import lax
from jax.experimental import pallas as pl
from jax.experimental.pallas import tpu as pltpu

F32 = jnp.float32
BF16 = jnp.bfloat16

EPS = 1e-6
D_MODEL = 2048
D_FF = 4 * D_MODEL
GLA_HEADS, GLA_DK, GLA_DV = 4, 128, 256
GLA_GATE_RANK = 16
GLA_GATE_TAU = 16.0
MOBA_HEADS, MOBA_HD, MOBA_BLOCK, MOBA_TOPK = 8, 128, 256, 3
FOX_HEADS, FOX_HD = 8, 128
HGRN_HEADS, HGRN_DK, HGRN_DV = 8, 128, 128
LINATTN_CHUNK = 64

LANES = 128
SUBLANES = 8
MIB = 1024 * 1024
VMEM_INTERNAL_SCRATCH = 8 * MIB

PROJ_TM, PROJ_TN = 512, 1024
OUT_TM = 256
MLP_TM, MLP_TF = 512, 512
LINATTN_TS = 512
ATTN_BLOCK = 256
CUMSUM_BLOCK = 256
LINATTN_SAFE_LOG_DECAY = 60.0
MASKED_LOGIT = -1e30

_NT = (((1,), (1,)), ((), ()))
_TN = (((0,), (0,)), ((), ()))


def _vmem_limit(pipelined_bytes, scratch_bytes=0):
    return int(2 * pipelined_bytes + scratch_bytes + VMEM_INTERNAL_SCRATCH)


def _nbytes(shape, dtype):
    n = 1
    for s in shape:
        n *= s
    return n * jnp.dtype(dtype).itemsize


def _sigmoid(x):
    return 1.0 / (1.0 + jnp.exp(-x))


def _log_sigmoid(x):
    return jnp.minimum(x, 0.0) - jnp.log1p(jnp.exp(-jnp.abs(x)))


def _rms_scale(x):
    return x * lax.rsqrt(jnp.mean(x * x, axis=-1, keepdims=True) + EPS)


def _norm_matmul_kernel(x_ref, g_ref, w_ref, ws_ref, o_ref, os_ref, xn_ref):
    @pl.when(pl.program_id(1) == 0)
    def _():
        xn = (_rms_scale(x_ref[...]) * g_ref[...]).astype(BF16)
        xn_ref[...] = xn
        os_ref[...] = jnp.dot(xn, ws_ref[...], preferred_element_type=F32)

    o_ref[...] = jnp.dot(xn_ref[...], w_ref[...], preferred_element_type=F32)


def _norm_matmul(x, g, w, ws):
    T, D = x.shape
    N = w.shape[1]
    tm, tn = min(PROJ_TM, T), PROJ_TN
    assert T % tm == 0 and N % tn == 0 and ws.shape == (D, LANES)
    blocks = (_nbytes((tm, D), F32) + _nbytes((D, tn), BF16) + _nbytes((D, LANES), BF16)
              + _nbytes((tm, tn), F32) + _nbytes((tm, LANES), F32))
    return pl.pallas_call(
        _norm_matmul_kernel,
        grid=(T // tm, N // tn),
        in_specs=[
            pl.BlockSpec((tm, D), lambda i, j: (i, 0)),
            pl.BlockSpec((1, D), lambda i, j: (0, 0)),
            pl.BlockSpec((D, tn), lambda i, j: (0, j)),
            pl.BlockSpec((D, LANES), lambda i, j: (0, 0)),
        ],
        out_specs=[
            pl.BlockSpec((tm, tn), lambda i, j: (i, j)),
            pl.BlockSpec((tm, LANES), lambda i, j: (i, 0)),
        ],
        out_shape=[jax.ShapeDtypeStruct((T, N), F32), jax.ShapeDtypeStruct((T, LANES), F32)],
        scratch_shapes=[pltpu.VMEM((tm, D), BF16)],
        compiler_params=pltpu.CompilerParams(
            dimension_semantics=("parallel", "arbitrary"),
            vmem_limit_bytes=_vmem_limit(blocks, _nbytes((tm, D), BF16))),
        name="norm_in_proj",
    )(x, g.reshape(1, D), w, ws)


def _out_proj_kernel(a_ref, b_ref, wa_ref, wb_ref, h_ref, o_ref):
    acc = jnp.dot(a_ref[...].astype(BF16), wa_ref[...], preferred_element_type=F32)
    acc += jnp.dot(b_ref[...].astype(BF16), wb_ref[...], preferred_element_type=F32)
    o_ref[...] = h_ref[...] + acc


def _out_proj(a, b, wa, wb, h):
    T, D = h.shape
    Ka, Kb = a.shape[1], b.shape[1]
    tm = min(OUT_TM, T)
    assert T % tm == 0
    blocks = (_nbytes((tm, Ka), F32) + _nbytes((tm, Kb), F32) + _nbytes((Ka, D), BF16)
              + _nbytes((Kb, D), BF16) + 2 * _nbytes((tm, D), F32))
    return pl.pallas_call(
        _out_proj_kernel,
        grid=(T // tm,),
        in_specs=[
            pl.BlockSpec((tm, Ka), lambda i: (i, 0)),
            pl.BlockSpec((tm, Kb), lambda i: (i, 0)),
            pl.BlockSpec((Ka, D), lambda i: (0, 0)),
            pl.BlockSpec((Kb, D), lambda i: (0, 0)),
            pl.BlockSpec((tm, D), lambda i: (i, 0)),
        ],
        out_specs=pl.BlockSpec((tm, D), lambda i: (i, 0)),
        out_shape=jax.ShapeDtypeStruct((T, D), F32),
        compiler_params=pltpu.CompilerParams(
            dimension_semantics=("parallel",), vmem_limit_bytes=_vmem_limit(blocks)),
        name="out_proj",
    )(a, b, wa, wb, h)


def _mlp_kernel(h_ref, g_ref, wu_ref, wd_ref, gf_ref, o_ref, xn_ref, *, final_norm):
    f = pl.program_id(1)

    @pl.when(f == 0)
    def _():
        x = h_ref[...]
        xn_ref[...] = (_rms_scale(x) * g_ref[...]).astype(BF16)
        o_ref[...] = x

    hid = jnp.dot(xn_ref[...], wu_ref[...], preferred_element_type=F32)
    hid = jnp.square(jnp.maximum(hid, 0.0))
    o_ref[...] += jnp.dot(hid.astype(BF16), wd_ref[...], preferred_element_type=F32)

    if final_norm:
        @pl.when(f == pl.num_programs(1) - 1)
        def _():
            o_ref[...] = _rms_scale(o_ref[...]) * gf_ref[...]


def _mlp(h, g, wu, wd, gf, final_norm):
    T, D = h.shape
    FF = wu.shape[1]
    tm, tf = min(MLP_TM, T), MLP_TF
    assert T % tm == 0 and FF % tf == 0
    blocks = (2 * _nbytes((tm, D), F32) + _nbytes((D, tf), BF16) + _nbytes((tf, D), BF16))
    return pl.pallas_call(
        functools.partial(_mlp_kernel, final_norm=final_norm),
        grid=(T // tm, FF // tf),
        in_specs=[
            pl.BlockSpec((tm, D), lambda i, f: (i, 0)),
            pl.BlockSpec((1, D), lambda i, f: (0, 0)),
            pl.BlockSpec((D, tf), lambda i, f: (0, f)),
            pl.BlockSpec((tf, D), lambda i, f: (f, 0)),
            pl.BlockSpec((1, D), lambda i, f: (0, 0)),
        ],
        out_specs=pl.BlockSpec((tm, D), lambda i, f: (i, 0)),
        out_shape=jax.ShapeDtypeStruct((T, D), F32),
        scratch_shapes=[pltpu.VMEM((tm, D), BF16)],
        compiler_params=pltpu.CompilerParams(
            dimension_semantics=("parallel", "arbitrary"),
            vmem_limit_bytes=_vmem_limit(blocks, _nbytes((tm, D), BF16))),
        name="mlp",
    )(h, g.reshape(1, D), wu, wd, gf.reshape(1, D))


def _linattn_kernel(*refs, mode, H, dk, dv, C, Ts):
    if mode == "gla":
        (q_ref, k_ref, v_ref, g_ref, glr_ref, w2_ref, b2_ref, gain_ref,
         o_ref, st_ref, b_sc, q_sc, k_sc) = refs
    else:
        (q_ref, k_ref, v_ref, g_ref, lb_ref, gain_ref,
         o_ref, st_ref, b_sc, q_sc, k_sc) = refs

    @pl.when(pl.program_id(1) == 0)
    def _():
        st_ref[...] = jnp.zeros_like(st_ref)

    if mode == "gla":
        z = jnp.dot(glr_ref[...].astype(BF16), w2_ref[...], preferred_element_type=F32) + b2_ref[...]
        log_decay = _log_sigmoid(z) * (1.0 / GLA_GATE_TAU)
        q_sc[...] = q_ref[...] * (dk ** -0.5)
        k_sc[...] = k_ref[...]
    else:
        lb = lb_ref[...]
        f = lb + (1.0 - lb) * _sigmoid(k_ref[...])
        log_decay = jnp.log(f)
        k_sc[...] = 1.0 - f
        x = q_ref[...]
        q_sc[...] = x * _sigmoid(x)

    pos = lax.broadcasted_iota(jnp.int32, log_decay.shape, 0) % C
    b_all = log_decay
    shift = 1
    while shift < C:
        b_all = b_all + jnp.where(pos >= shift, pltpu.roll(b_all, shift, axis=0), 0.0)
        shift *= 2
    b_sc[...] = b_all
    factorable = jnp.min(b_all) >= -LINATTN_SAFE_LOG_DECAY

    row = lax.broadcasted_iota(jnp.int32, (C, C), 0)
    col = lax.broadcasted_iota(jnp.int32, (C, C), 1)
    gain = gain_ref[...]

    def chunk(c, direct):
        r0 = pl.multiple_of(c * C, C)
        rows = pl.ds(r0, C)
        for h in range(H):
            ck = slice(h * dk, (h + 1) * dk)
            cv = slice(h * dv, (h + 1) * dv)
            b = b_sc[rows, ck]
            q = q_sc[rows, ck]
            k = k_sc[rows, ck]
            v = v_ref[rows, cv].astype(BF16)
            b_last = b[C - 1:C, :]
            qe = (q * jnp.exp(b)).astype(BF16)
            state = st_ref[h]
            o = lax.dot_general(qe, state.astype(BF16), _NT, preferred_element_type=F32)
            if direct:
                def columns(s8, scores):
                    r8 = pl.ds(pl.multiple_of(r0 + s8 * SUBLANES, SUBLANES), SUBLANES)
                    b8 = b_sc[r8, ck]
                    k8 = k_sc[r8, ck]
                    for i in range(SUBLANES):
                        decay = jnp.exp(b - b8[i:i + 1, :])
                        contrib = jnp.sum(q * (k8[i:i + 1, :] * decay), axis=-1, keepdims=True)
                        scores = jnp.where(col == s8 * SUBLANES + i, contrib, scores)
                    return scores
                scores = lax.fori_loop(0, C // SUBLANES, columns, jnp.zeros((C, C), F32))
            else:
                ke = (k * jnp.exp(-b)).astype(BF16)
                scores = lax.dot_general(qe, ke, _NT, preferred_element_type=F32)
            scores = jnp.where(row >= col, scores, 0.0)
            o = o + jnp.dot(scores.astype(BF16), v, preferred_element_type=F32)
            k_tail = (k * jnp.exp(b_last - b)).astype(BF16)
            st_ref[h] = state * jnp.exp(b_last) + lax.dot_general(
                v, k_tail, _TN, preferred_element_type=F32)
            g = g_ref[rows, cv]
            o_ref[rows, cv] = (_rms_scale(o) * gain) * (g * _sigmoid(g))

    n_chunks = Ts // C

    @pl.when(factorable)
    def _():
        lax.fori_loop(0, n_chunks, lambda c, _: (chunk(c, False), 0)[1], 0)

    @pl.when(jnp.logical_not(factorable))
    def _():
        lax.fori_loop(0, n_chunks, lambda c, _: (chunk(c, True), 0)[1], 0)


def _linattn(mode, u, col_blocks, extras, gain, B, S, H, dk, dv):
    C = LINATTN_CHUNK
    Ts = min(LINATTN_TS, S)
    assert S % Ts == 0 and Ts % C == 0
    wk, wv = H * dk, H * dv
    qi, ki, vi, gi = col_blocks
    in_specs = [
        pl.BlockSpec((None, Ts, wk), lambda b, s: (b, s, qi)),
        pl.BlockSpec((None, Ts, wk), lambda b, s: (b, s, ki)),
        pl.BlockSpec((None, Ts, wv), lambda b, s: (b, s, vi)),
        pl.BlockSpec((None, Ts, wv), lambda b, s: (b, s, gi)),
    ]
    args = [u, u, u, u]
    for e in extras:
        if e.ndim == 3:
            in_specs.append(pl.BlockSpec((None, Ts, e.shape[2]), lambda b, s: (b, s, 0)))
        else:
            in_specs.append(pl.BlockSpec(e.shape, lambda b, s: (0, 0)))
        args.append(e)
    in_specs.append(pl.BlockSpec((1, dv), lambda b, s: (0, 0)))
    args.append(gain.reshape(1, dv))
    blocks = (2 * _nbytes((Ts, wk), F32) + 3 * _nbytes((Ts, wv), F32) + _nbytes((Ts, LANES), F32)
              + _nbytes((LANES, wk), BF16))
    scratch = 3 * _nbytes((Ts, wk), F32) + _nbytes((H, dv, dk), F32)
    return pl.pallas_call(
        functools.partial(_linattn_kernel, mode=mode, H=H, dk=dk, dv=dv, C=C, Ts=Ts),
        grid=(B, S // Ts),
        in_specs=in_specs,
        out_specs=pl.BlockSpec((None, Ts, wv), lambda b, s: (b, s, 0)),
        out_shape=jax.ShapeDtypeStruct((B, S, wv), F32),
        scratch_shapes=[
            pltpu.VMEM((H, dv, dk), F32),
            pltpu.VMEM((Ts, wk), F32),
            pltpu.VMEM((Ts, wk), F32),
            pltpu.VMEM((Ts, wk), F32),
        ],
        compiler_params=pltpu.CompilerParams(
            dimension_semantics=("parallel", "arbitrary"),
            vmem_limit_bytes=_vmem_limit(blocks, scratch)),
        name="linattn_" + mode,
    )(*args)


def _fox_gate_kernel(ff_ref, bias_ref, c_ref, *, S, CB):
    r = lax.broadcasted_iota(jnp.int32, (CB, CB), 0)
    c = lax.broadcasted_iota(jnp.int32, (CB, CB), 1)
    tril = jnp.where(r >= c, 1.0, 0.0).astype(F32)
    carry = jnp.zeros((1, LANES), F32)
    for i in range(S // CB):
        rows = slice(i * CB, (i + 1) * CB)
        log_f = _log_sigmoid(ff_ref[rows, :] + bias_ref[...])
        cs = jnp.dot(tril, log_f, precision=lax.Precision.HIGHEST,
                     preferred_element_type=F32) + carry
        c_ref[rows, :] = cs
        carry = cs[CB - 1:CB, :]


def _fox_gate(ff, bias, B, S):
    CB = min(CUMSUM_BLOCK, S)
    assert S % CB == 0
    return pl.pallas_call(
        functools.partial(_fox_gate_kernel, S=S, CB=CB),
        grid=(B,),
        in_specs=[
            pl.BlockSpec((None, S, LANES), lambda b: (b, 0, 0)),
            pl.BlockSpec((1, LANES), lambda b: (0, 0)),
        ],
        out_specs=pl.BlockSpec((None, S, LANES), lambda b: (b, 0, 0)),
        out_shape=jax.ShapeDtypeStruct((B, S, LANES), F32),
        compiler_params=pltpu.CompilerParams(
            dimension_semantics=("parallel",),
            vmem_limit_bytes=_vmem_limit(2 * _nbytes((S, LANES), F32))),
        name="fox_gate",
    )(ff, bias)


def _attn_kernel(*refs, mode, d, BLK, nb, topk):
    if mode == "moba":
        q_ref, k_ref, v_ref, slope_ref, o_ref, m_sc, l_sc, acc_sc, kmean_sc = refs
    else:
        q_ref, k_ref, v_ref, ccol_ref, crow_ref, o_ref, m_sc, l_sc, acc_sc = refs
    h = pl.program_id(1)
    j = pl.program_id(2)

    q = q_ref[...] * (d ** -0.5)
    qb = q.astype(BF16)
    row = lax.broadcasted_iota(jnp.int32, (BLK, BLK), 0)
    col = lax.broadcasted_iota(jnp.int32, (BLK, BLK), 1)
    lane = lax.broadcasted_iota(jnp.int32, (BLK, LANES), 1)

    if mode == "moba":
        @pl.when(j == 0)
        def _():
            kmean_sc[...] = jnp.zeros_like(kmean_sc)
            for n in range(nb):
                kmean_sc[n:n + 1, :] = jnp.mean(k_ref[n * BLK:(n + 1) * BLK, :], axis=0, keepdims=True)

        gate = lax.dot_general(q, kmean_sc[...], _NT, precision=lax.Precision.HIGHEST,
                               preferred_element_type=F32)
        rank = jnp.zeros((BLK, LANES), F32)
        for m in range(nb):
            gm = gate[:, m:m + 1]
            beats = jnp.where(gm > gate, 1.0, jnp.where((gm == gate) & (lane > m), 1.0, 0.0))
            rank = rank + beats * (m < j).astype(F32)
        selected = jnp.where((lane < j) & (rank < topk), 1.0, 0.0)
        slope = slope_ref[:, :1]
        rel = (row - col).astype(F32)
    else:
        c_col = jnp.sum(jnp.where(lane == h, ccol_ref[...], 0.0), axis=-1, keepdims=True)

    def logits(n, k_blk):
        s = lax.dot_general(qb, k_blk.astype(BF16), _NT, preferred_element_type=F32)
        if mode == "moba":
            return s - slope * (rel + ((j - n) * BLK).astype(F32))
        return s + (c_col - crow_ref[pl.ds(n, 1), :])

    r0 = pl.multiple_of(j * BLK, BLK)
    s = jnp.where(row >= col, logits(j, k_ref[pl.ds(r0, BLK), :]), MASKED_LOGIT)
    m0 = jnp.max(s, axis=-1, keepdims=True)
    p = jnp.exp(s - m0)
    m_sc[...] = m0
    l_sc[...] = jnp.sum(p, axis=-1, keepdims=True)
    acc_sc[...] = jnp.dot(p.astype(BF16), v_ref[pl.ds(r0, BLK), :].astype(BF16),
                          preferred_element_type=F32)

    def past_block(n, carry):
        rn = pl.multiple_of(n * BLK, BLK)
        s = logits(n, k_ref[pl.ds(rn, BLK), :])
        if mode == "moba":
            keep = jnp.sum(jnp.where(lane == n, selected, 0.0), axis=-1, keepdims=True)
            s = jnp.where(keep > 0.5, s, MASKED_LOGIT)
        m_prev = m_sc[...]
        m_new = jnp.maximum(m_prev, jnp.max(s, axis=-1, keepdims=True))
        alpha = jnp.exp(m_prev - m_new)
        p = jnp.exp(s - m_new)
        l_sc[...] = alpha * l_sc[...] + jnp.sum(p, axis=-1, keepdims=True)
        acc_sc[...] = alpha * acc_sc[...] + jnp.dot(
            p.astype(BF16), v_ref[pl.ds(rn, BLK), :].astype(BF16), preferred_element_type=F32)
        m_sc[...] = m_new
        return carry

    lax.fori_loop(0, j, past_block, 0)
    o_ref[...] = acc_sc[...] / l_sc[...]


def _attention(mode, u, col_blocks, extras, B, S, H, d):
    BLK = min(ATTN_BLOCK, S)
    assert S % BLK == 0
    nb = S // BLK
    assert nb <= LANES
    qi, ki, vi = col_blocks
    in_specs = [
        pl.BlockSpec((None, BLK, d), lambda b, h, j: (b, j, qi + h)),
        pl.BlockSpec((None, S, d), lambda b, h, j: (b, 0, ki + h)),
        pl.BlockSpec((None, S, d), lambda b, h, j: (b, 0, vi + h)),
    ]
    scratch = [pltpu.VMEM((BLK, 1), F32), pltpu.VMEM((BLK, 1), F32), pltpu.VMEM((BLK, d), F32)]
    if mode == "moba":
        (slopes,) = extras
        in_specs.append(pl.BlockSpec((None, 1, LANES), lambda b, h, j: (h, 0, 0)))
        scratch.append(pltpu.VMEM((LANES, d), F32))
    else:
        c_cols, c_rows = extras
        in_specs.append(pl.BlockSpec((None, BLK, LANES), lambda b, h, j: (b, j, 0)))
        in_specs.append(pl.BlockSpec((None, None, nb, BLK), lambda b, h, j: (b, h, 0, 0)))
    blocks = 2 * _nbytes((S, d), F32) + 3 * _nbytes((BLK, LANES), F32) + _nbytes((nb, BLK), F32)
    return pl.pallas_call(
        functools.partial(_attn_kernel, mode=mode, d=d, BLK=BLK, nb=nb, topk=MOBA_TOPK),
        grid=(B, H, nb),
        in_specs=in_specs,
        out_specs=pl.BlockSpec((None, BLK, d), lambda b, h, j: (b, j, h)),
        out_shape=jax.ShapeDtypeStruct((B, S, H * d), F32),
        scratch_shapes=scratch,
        compiler_params=pltpu.CompilerParams(
            dimension_semantics=("parallel", "parallel", "arbitrary"),
            vmem_limit_bytes=_vmem_limit(blocks, 4 * _nbytes((BLK, LANES), F32))),
        name="attn_" + mode,
    )(u, u, u, *extras)


def _split_in_proj(w_in, lo, hi):
    main = jnp.concatenate([w_in[:, :lo], w_in[:, hi:]], axis=1).astype(BF16)
    side = jnp.pad(w_in[:, lo:hi], ((0, 0), (0, LANES - (hi - lo)))).astype(BF16)
    return main, side


def kernel(x, ev_norm_mix, ev_w_in, ev_gla_gate_w2, ev_gla_gate_b, ev_gla_out_norm, ev_w_out, ev_norm_mlp, ev_w_up, ev_w_down, od_norm_mix, od_w_in, od_fox_fgate_b, od_hgrn_out_norm, od_w_out, od_norm_mlp, od_w_up, od_w_down, hgrn_lb_raw, final_norm):
    B, S, D = x.shape
    T = B * S
    depth = hgrn_lb_raw.shape[0]
    lb_soft = jax.nn.softmax(hgrn_lb_raw.astype(F32), axis=0)
    lb_all = jnp.cumsum(lb_soft, axis=0) - lb_soft[0]
    slopes = jnp.exp2(-8.0 * jnp.arange(1, MOBA_HEADS + 1, dtype=F32) / MOBA_HEADS)
    slopes = jnp.broadcast_to(slopes[:, None, None], (MOBA_HEADS, 1, LANES))

    h = x.reshape(T, D)
    for layer in range(depth):
        e = layer // 2
        last = layer == depth - 1
        if layer % 2 == 0:
            gla_w = GLA_HEADS * GLA_DK
            lo = 2 * gla_w + 2 * GLA_HEADS * GLA_DV
            w_main, w_side = _split_in_proj(ev_w_in[e], lo, lo + GLA_GATE_RANK)
            u, glr = _norm_matmul(h, ev_norm_mix[e], w_main, w_side)
            u = u.reshape(B, S, -1)
            w2 = jnp.pad(ev_gla_gate_w2[e], ((0, LANES - GLA_GATE_RANK), (0, 0))).astype(BF16)
            mix_a = _linattn("gla", u, (0, 1, 1, 2),
                             [glr.reshape(B, S, LANES), w2, ev_gla_gate_b[e].reshape(1, gla_w)],
                             ev_gla_out_norm[e], B, S, GLA_HEADS, GLA_DK, GLA_DV)
            m0 = lo // MOBA_HD
            mix_b = _attention("moba", u, (m0, m0 + MOBA_HEADS, m0 + 2 * MOBA_HEADS), [slopes],
                               B, S, MOBA_HEADS, MOBA_HD)
            w_out, norm_mlp, w_up, w_down = ev_w_out[e], ev_norm_mlp[e], ev_w_up[e], ev_w_down[e]
        else:
            fox_w = FOX_HEADS * FOX_HD
            lo = 3 * fox_w
            w_main, w_side = _split_in_proj(od_w_in[e], lo, lo + FOX_HEADS)
            u, ff = _norm_matmul(h, od_norm_mix[e], w_main, w_side)
            u = u.reshape(B, S, -1)
            bias = jnp.pad(od_fox_fgate_b[e], (0, LANES - FOX_HEADS)).reshape(1, LANES)
            c_cols = _fox_gate(ff.reshape(B, S, LANES), bias, B, S)
            blk = min(ATTN_BLOCK, S)
            c_rows = c_cols[:, :, :FOX_HEADS].transpose(0, 2, 1).reshape(B, FOX_HEADS, S // blk, blk)
            mix_a = _attention("fox", u, (0, FOX_HEADS, 2 * FOX_HEADS), [c_cols, c_rows],
                               B, S, FOX_HEADS, FOX_HD)
            hw = HGRN_HEADS * HGRN_DK
            h0 = lo // hw
            mix_b = _linattn("hgrn", u, (h0, h0 + 1, h0 + 2, h0 + 3),
                             [lb_all[layer].reshape(1, hw)],
                             od_hgrn_out_norm[e], B, S, HGRN_HEADS, HGRN_DK, HGRN_DV)
            w_out, norm_mlp, w_up, w_down = od_w_out[e], od_norm_mlp[e], od_w_up[e], od_w_down[e]
        ka = mix_a.shape[-1]
        w_out = w_out.astype(BF16)
        h = _out_proj(mix_a.reshape(T, ka), mix_b.reshape(T, -1), w_out[:ka], w_out[ka:], h)
        h = _mlp(h, norm_mlp, w_up.astype(BF16), w_down.astype(BF16), final_norm, last)
    return h.reshape(B, S, D)
```

```python
import functools

import jax
import jax.numpy as jnp
from jax import lax
from jax.experimental import pallas as pl
from jax.experimental.pallas import tpu as pltpu

F32 = jnp.float32
BF16 = jnp.bfloat16

EPS = 1e-6
D_MODEL = 2048
D_FF = 4 * D_MODEL
GLA_HEADS, GLA_DK, GLA_DV = 4, 128, 256
GLA_GATE_RANK = 16
GLA_GATE_TAU = 16.0
MOBA_HEADS, MOBA_HD, MOBA_BLOCK, MOBA_TOPK = 8, 128, 256, 3
FOX_HEADS, FOX_HD = 8, 128
HGRN_HEADS, HGRN_DK, HGRN_DV = 8, 128, 128
LINATTN_CHUNK = 64

LANES = 128
SUBLANES = 8
MIB = 1024 * 1024
VMEM_INTERNAL_SCRATCH = 8 * MIB

PROJ_TM, PROJ_TN = 1024, 512
OUT_TM = 256
MLP_TM, MLP_TF = 1024, 512
LINATTN_TS = 512
ATTN_BLOCK = 256
ATTN_HEADS_PER_STEP = 4
CUMSUM_BLOCK = 256
LINATTN_SAFE_LOG_DECAY = 60.0
MASKED_LOGIT = -1e30

_NT = (((1,), (1,)), ((), ()))
_TN = (((0,), (0,)), ((), ()))


def _vmem_limit(pipelined_bytes, scratch_bytes=0):
    return int(2 * pipelined_bytes + scratch_bytes + VMEM_INTERNAL_SCRATCH)


def _nbytes(shape, dtype):
    n = 1
    for s in shape:
        n *= s
    return n * jnp.dtype(dtype).itemsize


def _sigmoid(x):
    return 1.0 / (1.0 + jnp.exp(-x))


def _log_sigmoid(x):
    return jnp.minimum(x, 0.0) - jnp.log1p(jnp.exp(-jnp.abs(x)))


def _rms_scale(x):
    return x * lax.rsqrt(jnp.mean(x * x, axis=-1, keepdims=True) + EPS)


def _norm_matmul_kernel(x_ref, g_ref, w_ref, ws_ref, o_ref, os_ref, xn_ref):
    @pl.when(pl.program_id(1) == 0)
    def _():
        xn = (_rms_scale(x_ref[...]) * g_ref[...]).astype(BF16)
        xn_ref[...] = xn
        os_ref[...] = jnp.dot(xn, ws_ref[...], preferred_element_type=F32)

    o_ref[...] = jnp.dot(xn_ref[...], w_ref[...], preferred_element_type=F32)


def _norm_matmul(x, g, w, ws):
    T, D = x.shape
    N = w.shape[1]
    tm, tn = min(PROJ_TM, T), PROJ_TN
    assert T % tm == 0 and N % tn == 0 and ws.shape == (D, LANES)
    blocks = (_nbytes((tm, D), F32) + _nbytes((D, tn), BF16) + _nbytes((D, LANES), BF16)
              + _nbytes((tm, tn), F32) + _nbytes((tm, LANES), F32))
    return pl.pallas_call(
        _norm_matmul_kernel,
        grid=(T // tm, N // tn),
        in_specs=[
            pl.BlockSpec((tm, D), lambda i, j: (i, 0)),
            pl.BlockSpec((1, D), lambda i, j: (0, 0)),
            pl.BlockSpec((D, tn), lambda i, j: (0, j)),
            pl.BlockSpec((D, LANES), lambda i, j: (0, 0)),
        ],
        out_specs=[
            pl.BlockSpec((tm, tn), lambda i, j: (i, j)),
            pl.BlockSpec((tm, LANES), lambda i, j: (i, 0)),
        ],
        out_shape=[jax.ShapeDtypeStruct((T, N), F32), jax.ShapeDtypeStruct((T, LANES), F32)],
        scratch_shapes=[pltpu.VMEM((tm, D), BF16)],
        compiler_params=pltpu.CompilerParams(
            dimension_semantics=("parallel", "arbitrary"),
            vmem_limit_bytes=_vmem_limit(blocks, _nbytes((tm, D), BF16))),
        name="norm_in_proj",
    )(x, g.reshape(1, D), w, ws)


def _out_proj_kernel(a_ref, b_ref, wa_ref, wb_ref, h_ref, o_ref):
    acc = jnp.dot(a_ref[...].astype(BF16), wa_ref[...], preferred_element_type=F32)
    acc += jnp.dot(b_ref[...].astype(BF16), wb_ref[...], preferred_element_type=F32)
    o_ref[...] = h_ref[...] + acc


def _out_proj(a, b, wa, wb, h):
    T, D = h.shape
    Ka, Kb = a.shape[1], b.shape[1]
    tm = min(OUT_TM, T)
    assert T % tm == 0
    blocks = (_nbytes((tm, Ka), F32) + _nbytes((tm, Kb), F32) + _nbytes((Ka, D), BF16)
              + _nbytes((Kb, D), BF16) + 2 * _nbytes((tm, D), F32))
    return pl.pallas_call(
        _out_proj_kernel,
        grid=(T // tm,),
        in_specs=[
            pl.BlockSpec((tm, Ka), lambda i: (i, 0)),
            pl.BlockSpec((tm, Kb), lambda i: (i, 0)),
            pl.BlockSpec((Ka, D), lambda i: (0, 0)),
            pl.BlockSpec((Kb, D), lambda i: (0, 0)),
            pl.BlockSpec((tm, D), lambda i: (i, 0)),
        ],
        out_specs=pl.BlockSpec((tm, D), lambda i: (i, 0)),
        out_shape=jax.ShapeDtypeStruct((T, D), F32),
        compiler_params=pltpu.CompilerParams(
            dimension_semantics=("parallel",), vmem_limit_bytes=_vmem_limit(blocks)),
        name="out_proj",
    )(a, b, wa, wb, h)


def _mlp_kernel(h_ref, g_ref, wu_ref, wd_ref, gf_ref, o_ref, xn_ref, *, final_norm):
    f = pl.program_id(1)

    @pl.when(f == 0)
    def _():
        x = h_ref[...]
        xn_ref[...] = (_rms_scale(x) * g_ref[...]).astype(BF16)
        o_ref[...] = x

    hid = jnp.dot(xn_ref[...], wu_ref[...], preferred_element_type=F32)
    hid = jnp.square(jnp.maximum(hid, 0.0))
    o_ref[...] += jnp.dot(hid.astype(BF16), wd_ref[...], preferred_element_type=F32)

    if final_norm:
        @pl.when(f == pl.num_programs(1) - 1)
        def _():
            o_ref[...] = _rms_scale(o_ref[...]) * gf_ref[...]


def _mlp(h, g, wu, wd, gf, final_norm):
    T, D = h.shape
    FF = wu.shape[1]
    tm, tf = min(MLP_TM, T), MLP_TF
    assert T % tm == 0 and FF % tf == 0
    blocks = (2 * _nbytes((tm, D), F32) + _nbytes((D, tf), BF16) + _nbytes((tf, D), BF16))
    return pl.pallas_call(
        functools.partial(_mlp_kernel, final_norm=final_norm),
        grid=(T // tm, FF // tf),
        in_specs=[
            pl.BlockSpec((tm, D), lambda i, f: (i, 0)),
            pl.BlockSpec((1, D), lambda i, f: (0, 0)),
            pl.BlockSpec((D, tf), lambda i, f: (0, f)),
            pl.BlockSpec((tf, D), lambda i, f: (f, 0)),
            pl.BlockSpec((1, D), lambda i, f: (0, 0)),
        ],
        out_specs=pl.BlockSpec((tm, D), lambda i, f: (i, 0)),
        out_shape=jax.ShapeDtypeStruct((T, D), F32),
        scratch_shapes=[pltpu.VMEM((tm, D), BF16)],
        compiler_params=pltpu.CompilerParams(
            dimension_semantics=("parallel", "arbitrary"),
            vmem_limit_bytes=_vmem_limit(blocks, _nbytes((tm, D), BF16))),
        name="mlp",
    )(h, g.reshape(1, D), wu, wd, gf.reshape(1, D))


def _linattn_kernel(*refs, mode, H, dk, dv, C, Ts):
    if mode == "gla":
        (q_ref, k_ref, v_ref, g_ref, glr_ref, w2_ref, b2_ref, gain_ref,
         o_ref, st_ref, b_sc, q_sc, k_sc) = refs
    else:
        (q_ref, k_ref, v_ref, g_ref, lb_ref, gain_ref,
         o_ref, st_ref, b_sc, q_sc, k_sc) = refs

    @pl.when(pl.program_id(1) == 0)
    def _():
        st_ref[...] = jnp.zeros_like(st_ref)

    n_chunks = Ts // C
    pos = lax.broadcasted_iota(jnp.int32, (C, H * dk), 0)

    def prepare(c, b_min):
        rows = pl.ds(pl.multiple_of(c * C, C), C)
        if mode == "gla":
            z = jnp.dot(glr_ref[rows, :].astype(BF16), w2_ref[...],
                        preferred_element_type=F32) + b2_ref[...]
            b = _log_sigmoid(z) * (1.0 / GLA_GATE_TAU)
            q_sc[rows, :] = q_ref[rows, :] * (dk ** -0.5)
            k_sc[rows, :] = k_ref[rows, :]
        else:
            lb = lb_ref[...]
            f = lb + (1.0 - lb) * _sigmoid(k_ref[rows, :])
            b = jnp.log(f)
            k_sc[rows, :] = 1.0 - f
            x = q_ref[rows, :]
            q_sc[rows, :] = x * _sigmoid(x)
        shift = 1
        while shift < C:
            b = b + jnp.where(pos >= shift, pltpu.roll(b, shift, axis=0), 0.0)
            shift *= 2
        b_sc[rows, :] = b
        return jnp.minimum(b_min, b[C - 1:C, :])

    b_min = lax.fori_loop(0, n_chunks, prepare, jnp.zeros((1, H * dk), F32))
    factorable = jnp.min(b_min) >= -LINATTN_SAFE_LOG_DECAY

    row = lax.broadcasted_iota(jnp.int32, (C, C), 0)
    col = lax.broadcasted_iota(jnp.int32, (C, C), 1)
    gain = gain_ref[...]

    def chunk(c, direct):
        r0 = pl.multiple_of(c * C, C)
        rows = pl.ds(r0, C)
        for h in range(H):
            ck = slice(h * dk, (h + 1) * dk)
            cv = slice(h * dv, (h + 1) * dv)
            b = b_sc[rows, ck]
            q = q_sc[rows, ck]
            k = k_sc[rows, ck]
            v = v_ref[rows, cv].astype(BF16)
            b_last = b[C - 1:C, :]
            qe = (q * jnp.exp(b)).astype(BF16)
            state = st_ref[h]
            o = lax.dot_general(qe, state.astype(BF16), _NT, preferred_element_type=F32)
            if direct:
                def columns(s8, scores):
                    r8 = pl.ds(pl.multiple_of(r0 + s8 * SUBLANES, SUBLANES), SUBLANES)
                    b8 = b_sc[r8, ck]
                    k8 = k_sc[r8, ck]
                    for i in range(SUBLANES):
                        decay = jnp.exp(b - b8[i:i + 1, :])
                        contrib = jnp.sum(q * (k8[i:i + 1, :] * decay), axis=-1, keepdims=True)
                        scores = jnp.where(col == s8 * SUBLANES + i, contrib, scores)
                    return scores
                scores = lax.fori_loop(0, C // SUBLANES, columns, jnp.zeros((C, C), F32))
            else:
                ke = (k * jnp.exp(-b)).astype(BF16)
                scores = lax.dot_general(qe, ke, _NT, preferred_element_type=F32)
            scores = jnp.where(row >= col, scores, 0.0)
            o = o + jnp.dot(scores.astype(BF16), v, preferred_element_type=F32)
            k_tail = (k * jnp.exp(b_last - b)).astype(BF16)
            st_ref[h] = state * jnp.exp(b_last) + lax.dot_general(
                v, k_tail, _TN, preferred_element_type=F32)
            g = g_ref[rows, cv]
            o_ref[rows, cv] = (_rms_scale(o) * gain) * (g * _sigmoid(g))

    @pl.when(factorable)
    def _():
        lax.fori_loop(0, n_chunks, lambda c, _: (chunk(c, False), 0)[1], 0)

    @pl.when(jnp.logical_not(factorable))
    def _():
        lax.fori_loop(0, n_chunks, lambda c, _: (chunk(c, True), 0)[1], 0)


def _linattn(mode, u, col_blocks, extras, gain, B, S, H, dk, dv):
    C = LINATTN_CHUNK
    Ts = min(LINATTN_TS, S)
    assert S % Ts == 0 and Ts % C == 0
    wk, wv = H * dk, H * dv
    qi, ki, vi, gi = col_blocks
    in_specs = [
        pl.BlockSpec((None, Ts, wk), lambda b, s: (b, s, qi)),
        pl.BlockSpec((None, Ts, wk), lambda b, s: (b, s, ki)),
        pl.BlockSpec((None, Ts, wv), lambda b, s: (b, s, vi)),
        pl.BlockSpec((None, Ts, wv), lambda b, s: (b, s, gi)),
    ]
    args = [u, u, u, u]
    for e in extras:
        if e.ndim == 3:
            in_specs.append(pl.BlockSpec((None, Ts, e.shape[2]), lambda b, s: (b, s, 0)))
        else:
            in_specs.append(pl.BlockSpec(e.shape, lambda b, s: (0, 0)))
        args.append(e)
    in_specs.append(pl.BlockSpec((1, dv), lambda b, s: (0, 0)))
    args.append(gain.reshape(1, dv))
    blocks = (2 * _nbytes((Ts, wk), F32) + 3 * _nbytes((Ts, wv), F32) + _nbytes((Ts, LANES), F32)
              + _nbytes((LANES, wk), BF16))
    scratch = 3 * _nbytes((Ts, wk), F32) + _nbytes((H, dv, dk), F32)
    return pl.pallas_call(
        functools.partial(_linattn_kernel, mode=mode, H=H, dk=dk, dv=dv, C=C, Ts=Ts),
        grid=(B, S // Ts),
        in_specs=in_specs,
        out_specs=pl.BlockSpec((None, Ts, wv), lambda b, s: (b, s, 0)),
        out_shape=jax.ShapeDtypeStruct((B, S, wv), F32),
        scratch_shapes=[
            pltpu.VMEM((H, dv, dk), F32),
            pltpu.VMEM((Ts, wk), F32),
            pltpu.VMEM((Ts, wk), F32),
            pltpu.VMEM((Ts, wk), F32),
        ],
        compiler_params=pltpu.CompilerParams(
            dimension_semantics=("parallel", "arbitrary"),
            vmem_limit_bytes=_vmem_limit(blocks, scratch)),
        name="linattn_" + mode,
    )(*args)


def _fox_gate_kernel(ff_ref, bias_ref, c_ref, *, S, CB):
    r = lax.broadcasted_iota(jnp.int32, (CB, CB), 0)
    c = lax.broadcasted_iota(jnp.int32, (CB, CB), 1)
    tril = jnp.where(r >= c, 1.0, 0.0).astype(F32)
    carry = jnp.zeros((1, LANES), F32)
    for i in range(S // CB):
        rows = slice(i * CB, (i + 1) * CB)
        log_f = _log_sigmoid(ff_ref[rows, :] + bias_ref[...])
        cs = jnp.dot(tril, log_f, precision=lax.Precision.HIGHEST,
                     preferred_element_type=F32) + carry
        c_ref[rows, :] = cs
        carry = cs[CB - 1:CB, :]


def _fox_gate(ff, bias, B, S):
    CB = min(CUMSUM_BLOCK, S)
    assert S % CB == 0
    return pl.pallas_call(
        functools.partial(_fox_gate_kernel, S=S, CB=CB),
        grid=(B,),
        in_specs=[
            pl.BlockSpec((None, S, LANES), lambda b: (b, 0, 0)),
            pl.BlockSpec((1, LANES), lambda b: (0, 0)),
        ],
        out_specs=pl.BlockSpec((None, S, LANES), lambda b: (b, 0, 0)),
        out_shape=jax.ShapeDtypeStruct((B, S, LANES), F32),
        compiler_params=pltpu.CompilerParams(
            dimension_semantics=("parallel",),
            vmem_limit_bytes=_vmem_limit(2 * _nbytes((S, LANES), F32))),
        name="fox_gate",
    )(ff, bias)


def _attn_kernel(*refs, mode, d, BLK, nb, topk, G):
    if mode == "moba":
        q_ref, k_ref, v_ref, slope_ref, o_ref, kb_sc, vt_sc, acc_sc, kmean_sc = refs
    else:
        q_ref, k_ref, v_ref, ccol_ref, crow_ref, o_ref, kb_sc, vt_sc, acc_sc, crep_sc = refs
    hg = pl.program_id(1)
    j = pl.program_id(2)
    S = nb * BLK

    @pl.when(j == 0)
    def _():
        for g in range(G):
            cd = slice(g * d, (g + 1) * d)
            if mode == "moba":
                kmean_sc[g] = jnp.zeros(kmean_sc.shape[1:], F32)
            else:
                lane = lax.broadcasted_iota(jnp.int32, (S, LANES), 1)
                c_key = jnp.sum(jnp.where(lane == hg * G + g, ccol_ref[...], 0.0),
                                axis=-1, keepdims=True)
            for n in range(nb):
                rows = slice(n * BLK, (n + 1) * BLK)
                if mode == "fox":
                    crep_sc[g, n] = jnp.broadcast_to(c_key[rows], (BLK, LANES))
                k_blk = k_ref[rows, cd]
                kb_sc[g, n] = k_blk.astype(BF16)
                vt_sc[g, n] = v_ref[rows, cd].T.astype(BF16)
                if mode == "moba":
                    kmean_sc[g, n:n + 1, :] = jnp.mean(k_blk, axis=0, keepdims=True)

    key_i = lax.broadcasted_iota(jnp.int32, (BLK, BLK), 0)
    qry_i = lax.broadcasted_iota(jnp.int32, (BLK, BLK), 1)
    causal = qry_i >= key_i
    qb, slope, selected = [], [], []
    if mode == "moba":
        rel = (qry_i - key_i).astype(F32)
        blk_i = lax.broadcasted_iota(jnp.int32, (kmean_sc.shape[1], BLK), 0)

    for g in range(G):
        q = q_ref[:, g * d:(g + 1) * d] * (d ** -0.5)
        qb.append(q.astype(BF16))
        if mode == "moba":
            gate = lax.dot_general(kmean_sc[g], q, _NT, precision=lax.Precision.HIGHEST,
                                   preferred_element_type=F32)
            rank = jnp.zeros(gate.shape, F32)
            for m in range(nb):
                gm = gate[m:m + 1, :]
                beats = jnp.where(gm > gate, 1.0,
                                  jnp.where((gm == gate) & (blk_i > m), 1.0, 0.0))
                rank = rank + beats * (m < j).astype(F32)
            selected.append(jnp.where((blk_i < j) & (rank < topk), 1.0, 0.0))
            slope.append(slope_ref[g, :, :1])

    def logits(g, n):
        s = lax.dot_general(kb_sc[g, n], qb[g], _NT, preferred_element_type=F32)
        if mode == "moba":
            return s - slope[g] * (rel + ((j - n) * BLK).astype(F32))
        c_key = crep_sc[g, n]
        c_qry = crow_ref[g, pl.ds(j, 1), :]
        return s + (c_qry - jnp.concatenate([c_key] * (BLK // LANES), axis=1))

    scores = [jnp.where(causal, logits(g, j), MASKED_LOGIT) for g in range(G)]
    stats, probs = [], []
    for g in range(G):
        m0 = jnp.max(scores[g], axis=0, keepdims=True)
        p = jnp.exp(scores[g] - m0)
        stats += [m0, jnp.sum(p, axis=0, keepdims=True)]
        probs.append(p.astype(BF16))
    for g in range(G):
        acc_sc[g] = jnp.dot(vt_sc[g, j], probs[g], preferred_element_type=F32)

    def past_block(n, stats):
        scores = [logits(g, n) for g in range(G)]
        out, probs, alphas = [], [], []
        for g in range(G):
            m_prev, l_prev = stats[2 * g], stats[2 * g + 1]
            s = scores[g]
            if mode == "moba":
                keep = jnp.sum(jnp.where(blk_i == n, selected[g], 0.0), axis=0, keepdims=True)
                s = jnp.where(keep > 0.5, s, MASKED_LOGIT)
            m_new = jnp.maximum(m_prev, jnp.max(s, axis=0, keepdims=True))
            alpha = jnp.exp(m_prev - m_new)
            p = jnp.exp(s - m_new)
            out += [m_new, alpha * l_prev + jnp.sum(p, axis=0, keepdims=True)]
            probs.append(p.astype(BF16))
            alphas.append(alpha)
        updates = [jnp.dot(vt_sc[g, n], probs[g], preferred_element_type=F32) for g in range(G)]
        for g in range(G):
            acc_sc[g] = alphas[g] * acc_sc[g] + updates[g]
        return tuple(out)

    stats = lax.fori_loop(0, j, past_block, tuple(stats))
    for g in range(G):
        o_ref[:, g * d:(g + 1) * d] = (acc_sc[g] / stats[2 * g + 1]).T


def _attention(mode, u, col_blocks, extras, B, S, H, d):
    BLK = min(ATTN_BLOCK, S)
    G = ATTN_HEADS_PER_STEP
    assert S % BLK == 0 and BLK % LANES == 0 and H % G == 0
    nb = S // BLK
    nbp = -(-nb // SUBLANES) * SUBLANES
    qi, ki, vi = (c // G for c in col_blocks)
    assert all(c % G == 0 for c in col_blocks)
    w = G * d
    in_specs = [
        pl.BlockSpec((None, BLK, w), lambda b, h, j: (b, j, qi + h)),
        pl.BlockSpec((None, S, w), lambda b, h, j: (b, 0, ki + h)),
        pl.BlockSpec((None, S, w), lambda b, h, j: (b, 0, vi + h)),
    ]
    scratch = [pltpu.VMEM((G, nb, BLK, d), BF16), pltpu.VMEM((G, nb, d, BLK), BF16),
               pltpu.VMEM((G, d, BLK), F32)]
    scratch_bytes = 2 * _nbytes((G, S, d), BF16) + _nbytes((G, d, BLK), F32)
    if mode == "moba":
        (slopes,) = extras
        in_specs.append(pl.BlockSpec((G, 1, LANES), lambda b, h, j: (h, 0, 0)))
        scratch.append(pltpu.VMEM((G, nbp, d), F32))
        blocks = 2 * _nbytes((S, w), F32) + 2 * _nbytes((BLK, w), F32)
    else:
        c_cols, c_rows = extras
        in_specs.append(pl.BlockSpec((None, S, LANES), lambda b, h, j: (b, 0, 0)))
        in_specs.append(pl.BlockSpec((None, G, nb, BLK), lambda b, h, j: (b, h, 0, 0)))
        scratch.append(pltpu.VMEM((G, nb, BLK, LANES), F32))
        scratch_bytes += _nbytes((G, S, LANES), F32)
        blocks = (2 * _nbytes((S, w), F32) + 2 * _nbytes((BLK, w), F32)
                  + _nbytes((S, LANES), F32) + _nbytes((G, nb, BLK), F32))
    return pl.pallas_call(
        functools.partial(_attn_kernel, mode=mode, d=d, BLK=BLK, nb=nb, topk=MOBA_TOPK, G=G),
        grid=(B, H // G, nb),
        in_specs=in_specs,
        out_specs=pl.BlockSpec((None, BLK, w), lambda b, h, j: (b, j, h)),
        out_shape=jax.ShapeDtypeStruct((B, S, H * d), F32),
        scratch_shapes=scratch,
        compiler_params=pltpu.CompilerParams(
            dimension_semantics=("parallel", "parallel", "arbitrary"),
            vmem_limit_bytes=_vmem_limit(blocks, scratch_bytes)),
        name="attn_" + mode,
    )(u, u, u, *extras)


def _split_in_proj(w_in, lo, hi):
    main = jnp.concatenate([w_in[:, :lo], w_in[:, hi:]], axis=1).astype(BF16)
    side = jnp.pad(w_in[:, lo:hi], ((0, 0), (0, LANES - (hi - lo)))).astype(BF16)
    return main, side


def kernel(x, ev_norm_mix, ev_w_in, ev_gla_gate_w2, ev_gla_gate_b, ev_gla_out_norm, ev_w_out, ev_norm_mlp, ev_w_up, ev_w_down, od_norm_mix, od_w_in, od_fox_fgate_b, od_hgrn_out_norm, od_w_out, od_norm_mlp, od_w_up, od_w_down, hgrn_lb_raw, final_norm):
    B, S, D = x.shape
    T = B * S
    depth = hgrn_lb_raw.shape[0]
    lb_soft = jax.nn.softmax(hgrn_lb_raw.astype(F32), axis=0)
    lb_all = jnp.cumsum(lb_soft, axis=0) - lb_soft[0]
    slopes = jnp.exp2(-8.0 * jnp.arange(1, MOBA_HEADS + 1, dtype=F32) / MOBA_HEADS)
    slopes = jnp.broadcast_to(slopes[:, None, None], (MOBA_HEADS, 1, LANES))

    h = x.reshape(T, D)
    for layer in range(depth):
        e = layer // 2
        last = layer == depth - 1
        if layer % 2 == 0:
            gla_w = GLA_HEADS * GLA_DK
            lo = 2 * gla_w + 2 * GLA_HEADS * GLA_DV
            w_main, w_side = _split_in_proj(ev_w_in[e], lo, lo + GLA_GATE_RANK)
            u, glr = _norm_matmul(h, ev_norm_mix[e], w_main, w_side)
            u = u.reshape(B, S, -1)
            w2 = jnp.pad(ev_gla_gate_w2[e], ((0, LANES - GLA_GATE_RANK), (0, 0))).astype(BF16)
            mix_a = _linattn("gla", u, (0, 1, 1, 2),
                             [glr.reshape(B, S, LANES), w2, ev_gla_gate_b[e].reshape(1, gla_w)],
                             ev_gla_out_norm[e], B, S, GLA_HEADS, GLA_DK, GLA_DV)
            m0 = lo // MOBA_HD
            mix_b = _attention("moba", u, (m0, m0 + MOBA_HEADS, m0 + 2 * MOBA_HEADS), [slopes],
                               B, S, MOBA_HEADS, MOBA_HD)
            w_out, norm_mlp, w_up, w_down = ev_w_out[e], ev_norm_mlp[e], ev_w_up[e], ev_w_down[e]
        else:
            fox_w = FOX_HEADS * FOX_HD
            lo = 3 * fox_w
            w_main, w_side = _split_in_proj(od_w_in[e], lo, lo + FOX_HEADS)
            u, ff = _norm_matmul(h, od_norm_mix[e], w_main, w_side)
            u = u.reshape(B, S, -1)
            bias = jnp.pad(od_fox_fgate_b[e], (0, LANES - FOX_HEADS)).reshape(1, LANES)
            c_cols = _fox_gate(ff.reshape(B, S, LANES), bias, B, S)
            blk = min(ATTN_BLOCK, S)
            c_rows = c_cols[:, :, :FOX_HEADS].transpose(0, 2, 1).reshape(B, FOX_HEADS, S // blk, blk)
            mix_a = _attention("fox", u, (0, FOX_HEADS, 2 * FOX_HEADS), [c_cols, c_rows],
                               B, S, FOX_HEADS, FOX_HD)
            hw = HGRN_HEADS * HGRN_DK
            h0 = lo // hw
            mix_b = _linattn("hgrn", u, (h0, h0 + 1, h0 + 2, h0 + 3),
                             [lb_all[layer].reshape(1, hw)],
                             od_hgrn_out_norm[e], B, S, HGRN_HEADS, HGRN_DK, HGRN_DV)
            w_out, norm_mlp, w_up, w_down = od_w_out[e], od_norm_mlp[e], od_w_up[e], od_w_down[e]
        ka = mix_a.shape[-1]
        w_out = w_out.astype(BF16)
        h = _out_proj(mix_a.reshape(T, ka), mix_b.reshape(T, -1), w_out[:ka], w_out[ka:], h)
        h = _mlp(h, norm_mlp, w_up.astype(BF16), w_down.astype(BF16), final_norm, last)
    return h.reshape(B, S, D)
```

```python
import functools

import jax
import jax.numpy as jnp
from jax import lax
from jax.experimental import pallas as pl
from jax.experimental.pallas import tpu as pltpu

F32 = jnp.float32
BF16 = jnp.bfloat16

EPS = 1e-6
D_MODEL = 2048
D_FF = 4 * D_MODEL
GLA_HEADS, GLA_DK, GLA_DV = 4, 128, 256
GLA_GATE_RANK = 16
GLA_GATE_TAU = 16.0
MOBA_HEADS, MOBA_HD, MOBA_BLOCK, MOBA_TOPK = 8, 128, 256, 3
FOX_HEADS, FOX_HD = 8, 128
HGRN_HEADS, HGRN_DK, HGRN_DV = 8, 128, 128
LINATTN_CHUNK = 64

LANES = 128
SUBLANES = 8
MIB = 1024 * 1024
VMEM_INTERNAL_SCRATCH = 8 * MIB

PROJ_TM, PROJ_TN = 1024, 1024
OUT_TM = 256
MLP_TM, MLP_TF = 1024, 512
LINATTN_TS = 512
ATTN_BLOCK = 256
ATTN_HEADS_PER_STEP = 4
CUMSUM_BLOCK = 256
LINATTN_SAFE_LOG_DECAY = 60.0
MASKED_LOGIT = -1e30

_NT = (((1,), (1,)), ((), ()))
_TN = (((0,), (0,)), ((), ()))


def _vmem_limit(pipelined_bytes, scratch_bytes=0):
    return int(2 * pipelined_bytes + scratch_bytes + VMEM_INTERNAL_SCRATCH)


def _nbytes(shape, dtype):
    n = 1
    for s in shape:
        n *= s
    return n * jnp.dtype(dtype).itemsize


def _sigmoid(x):
    return 1.0 / (1.0 + jnp.exp(-x))


def _log_sigmoid(x):
    return jnp.minimum(x, 0.0) - jnp.log1p(jnp.exp(-jnp.abs(x)))


def _rms_scale(x):
    return x * lax.rsqrt(jnp.mean(x * x, axis=-1, keepdims=True) + EPS)


def _norm_matmul_kernel(x_ref, g_ref, w_ref, ws_ref, o_ref, os_ref, xn_ref):
    @pl.when(pl.program_id(1) == 0)
    def _():
        xn = (_rms_scale(x_ref[...]) * g_ref[...]).astype(BF16)
        xn_ref[...] = xn
        os_ref[...] = jnp.dot(xn, ws_ref[...], preferred_element_type=F32)

    o_ref[...] = jnp.dot(xn_ref[...], w_ref[...], preferred_element_type=F32)


def _norm_matmul(x, g, w, ws):
    T, D = x.shape
    N = w.shape[1]
    tm, tn = min(PROJ_TM, T), PROJ_TN
    assert T % tm == 0 and N % tn == 0 and ws.shape == (D, LANES)
    blocks = (_nbytes((tm, D), F32) + _nbytes((D, tn), BF16) + _nbytes((D, LANES), BF16)
              + _nbytes((tm, tn), F32) + _nbytes((tm, LANES), F32))
    return pl.pallas_call(
        _norm_matmul_kernel,
        grid=(T // tm, N // tn),
        in_specs=[
            pl.BlockSpec((tm, D), lambda i, j: (i, 0)),
            pl.BlockSpec((1, D), lambda i, j: (0, 0)),
            pl.BlockSpec((D, tn), lambda i, j: (0, j)),
            pl.BlockSpec((D, LANES), lambda i, j: (0, 0)),
        ],
        out_specs=[
            pl.BlockSpec((tm, tn), lambda i, j: (i, j)),
            pl.BlockSpec((tm, LANES), lambda i, j: (i, 0)),
        ],
        out_shape=[jax.ShapeDtypeStruct((T, N), F32), jax.ShapeDtypeStruct((T, LANES), F32)],
        scratch_shapes=[pltpu.VMEM((tm, D), BF16)],
        compiler_params=pltpu.CompilerParams(
            dimension_semantics=("parallel", "arbitrary"),
            vmem_limit_bytes=_vmem_limit(blocks, _nbytes((tm, D), BF16))),
        name="norm_in_proj",
    )(x, g.reshape(1, D), w, ws)


def _out_proj_kernel(a_ref, b_ref, wa_ref, wb_ref, h_ref, o_ref):
    acc = jnp.dot(a_ref[...].astype(BF16), wa_ref[...], preferred_element_type=F32)
    acc += jnp.dot(b_ref[...].astype(BF16), wb_ref[...], preferred_element_type=F32)
    o_ref[...] = h_ref[...] + acc


def _out_proj(a, b, w, h):
    T, D = h.shape
    Ka, Kb = a.shape[1], b.shape[1]
    tm = min(OUT_TM, T)
    assert T % tm == 0 and Ka == Kb and w.shape == (Ka + Kb, D)
    blocks = (_nbytes((tm, Ka), F32) + _nbytes((tm, Kb), F32) + _nbytes((Ka, D), BF16)
              + _nbytes((Kb, D), BF16) + 2 * _nbytes((tm, D), F32))
    return pl.pallas_call(
        _out_proj_kernel,
        grid=(T // tm,),
        in_specs=[
            pl.BlockSpec((tm, Ka), lambda i: (i, 0)),
            pl.BlockSpec((tm, Kb), lambda i: (i, 0)),
            pl.BlockSpec((Ka, D), lambda i: (0, 0)),
            pl.BlockSpec((Kb, D), lambda i: (1, 0)),
            pl.BlockSpec((tm, D), lambda i: (i, 0)),
        ],
        out_specs=pl.BlockSpec((tm, D), lambda i: (i, 0)),
        out_shape=jax.ShapeDtypeStruct((T, D), F32),
        compiler_params=pltpu.CompilerParams(
            dimension_semantics=("parallel",), vmem_limit_bytes=_vmem_limit(blocks)),
        name="out_proj",
    )(a, b, w, w, h)


def _mlp_kernel(h_ref, g_ref, wu_ref, wd_ref, gf_ref, o_ref, xn_ref, *, final_norm):
    f = pl.program_id(1)

    @pl.when(f == 0)
    def _():
        x = h_ref[...]
        xn_ref[...] = (_rms_scale(x) * g_ref[...]).astype(BF16)
        o_ref[...] = x

    hid = jnp.dot(xn_ref[...], wu_ref[...], preferred_element_type=F32)
    hid = jnp.square(jnp.maximum(hid, 0.0))
    o_ref[...] += jnp.dot(hid.astype(BF16), wd_ref[...], preferred_element_type=F32)

    if final_norm:
        @pl.when(f == pl.num_programs(1) - 1)
        def _():
            o_ref[...] = _rms_scale(o_ref[...]) * gf_ref[...]


def _mlp(h, g, wu, wd, gf, final_norm):
    T, D = h.shape
    FF = wu.shape[1]
    tm, tf = min(MLP_TM, T), MLP_TF
    assert T % tm == 0 and FF % tf == 0
    blocks = (2 * _nbytes((tm, D), F32) + _nbytes((D, tf), BF16) + _nbytes((tf, D), BF16))
    return pl.pallas_call(
        functools.partial(_mlp_kernel, final_norm=final_norm),
        grid=(T // tm, FF // tf),
        in_specs=[
            pl.BlockSpec((tm, D), lambda i, f: (i, 0)),
            pl.BlockSpec((1, D), lambda i, f: (0, 0)),
            pl.BlockSpec((D, tf), lambda i, f: (0, f)),
            pl.BlockSpec((tf, D), lambda i, f: (f, 0)),
            pl.BlockSpec((1, D), lambda i, f: (0, 0)),
        ],
        out_specs=pl.BlockSpec((tm, D), lambda i, f: (i, 0)),
        out_shape=jax.ShapeDtypeStruct((T, D), F32),
        scratch_shapes=[pltpu.VMEM((tm, D), BF16)],
        compiler_params=pltpu.CompilerParams(
            dimension_semantics=("parallel", "arbitrary"),
            vmem_limit_bytes=_vmem_limit(blocks, _nbytes((tm, D), BF16))),
        name="mlp",
    )(h, g.reshape(1, D), wu, wd, gf.reshape(1, D))


def _linattn_kernel(*refs, mode, H, dk, dv, C, Ts):
    if mode == "gla":
        (q_ref, k_ref, v_ref, g_ref, glr_ref, w2_ref, b2_ref, gain_ref,
         o_ref, st_ref, b_sc, q_sc, k_sc) = refs
    else:
        (q_ref, k_ref, v_ref, g_ref, lb_ref, gain_ref,
         o_ref, st_ref, b_sc, q_sc, k_sc) = refs

    @pl.when(pl.program_id(1) == 0)
    def _():
        st_ref[...] = jnp.zeros_like(st_ref)

    n_chunks = Ts // C
    pos = lax.broadcasted_iota(jnp.int32, (C, H * dk), 0)

    def prepare(c, b_min):
        rows = pl.ds(pl.multiple_of(c * C, C), C)
        if mode == "gla":
            z = jnp.dot(glr_ref[rows, :].astype(BF16), w2_ref[...],
                        preferred_element_type=F32) + b2_ref[...]
            b = _log_sigmoid(z) * (1.0 / GLA_GATE_TAU)
            q_sc[rows, :] = q_ref[rows, :] * (dk ** -0.5)
            k_sc[rows, :] = k_ref[rows, :]
        else:
            lb = lb_ref[...]
            f = lb + (1.0 - lb) * _sigmoid(k_ref[rows, :])
            b = jnp.log(f)
            k_sc[rows, :] = 1.0 - f
            x = q_ref[rows, :]
            q_sc[rows, :] = x * _sigmoid(x)
        shift = 1
        while shift < C:
            b = b + jnp.where(pos >= shift, pltpu.roll(b, shift, axis=0), 0.0)
            shift *= 2
        b_sc[rows, :] = b
        return jnp.minimum(b_min, b[C - 1:C, :])

    b_min = lax.fori_loop(0, n_chunks, prepare, jnp.zeros((1, H * dk), F32))
    factorable = jnp.min(b_min) >= -LINATTN_SAFE_LOG_DECAY

    row = lax.broadcasted_iota(jnp.int32, (C, C), 0)
    col = lax.broadcasted_iota(jnp.int32, (C, C), 1)
    gain = gain_ref[...]

    def chunk(c, direct):
        r0 = pl.multiple_of(c * C, C)
        rows = pl.ds(r0, C)
        inter, scores, vs, new_state = [], [], [], []
        for h in range(H):
            ck = slice(h * dk, (h + 1) * dk)
            b = b_sc[rows, ck]
            q = q_sc[rows, ck]
            k = k_sc[rows, ck]
            v = v_ref[rows, h * dv:(h + 1) * dv].astype(BF16)
            b_last = b[C - 1:C, :]
            qe = (q * jnp.exp(b)).astype(BF16)
            state = st_ref[h]
            inter.append(lax.dot_general(qe, state.astype(BF16), _NT, preferred_element_type=F32))
            if direct:
                def columns(s8, sc, ck=ck, b=b, q=q):
                    r8 = pl.ds(pl.multiple_of(r0 + s8 * SUBLANES, SUBLANES), SUBLANES)
                    b8 = b_sc[r8, ck]
                    k8 = k_sc[r8, ck]
                    for i in range(SUBLANES):
                        decay = jnp.exp(b - b8[i:i + 1, :])
                        contrib = jnp.sum(q * (k8[i:i + 1, :] * decay), axis=-1, keepdims=True)
                        sc = jnp.where(col == s8 * SUBLANES + i, contrib, sc)
                    return sc
                sc = lax.fori_loop(0, C // SUBLANES, columns, jnp.zeros((C, C), F32))
            else:
                ke = (k * jnp.exp(-b)).astype(BF16)
                sc = lax.dot_general(qe, ke, _NT, preferred_element_type=F32)
            scores.append(sc)
            k_tail = (k * jnp.exp(b_last - b)).astype(BF16)
            new_state.append(state * jnp.exp(b_last) + lax.dot_general(
                v, k_tail, _TN, preferred_element_type=F32))
            vs.append(v)
        outs = []
        for h in range(H):
            sc = jnp.where(row >= col, scores[h], 0.0).astype(BF16)
            outs.append(inter[h] + jnp.dot(sc, vs[h], preferred_element_type=F32))
        for h in range(H):
            cv = slice(h * dv, (h + 1) * dv)
            st_ref[h] = new_state[h]
            g = g_ref[rows, cv]
            o_ref[rows, cv] = (_rms_scale(outs[h]) * gain) * (g * _sigmoid(g))

    @pl.when(factorable)
    def _():
        lax.fori_loop(0, n_chunks, lambda c, _: (chunk(c, False), 0)[1], 0)

    @pl.when(jnp.logical_not(factorable))
    def _():
        lax.fori_loop(0, n_chunks, lambda c, _: (chunk(c, True), 0)[1], 0)


def _linattn(mode, u, col_blocks, extras, gain, B, S, H, dk, dv):
    C = LINATTN_CHUNK
    Ts = min(LINATTN_TS, S)
    assert S % Ts == 0 and Ts % C == 0
    wk, wv = H * dk, H * dv
    qi, ki, vi, gi = col_blocks
    in_specs = [
        pl.BlockSpec((None, Ts, wk), lambda b, s: (b, s, qi)),
        pl.BlockSpec((None, Ts, wk), lambda b, s: (b, s, ki)),
        pl.BlockSpec((None, Ts, wv), lambda b, s: (b, s, vi)),
        pl.BlockSpec((None, Ts, wv), lambda b, s: (b, s, gi)),
    ]
    args = [u, u, u, u]
    for e in extras:
        if e.ndim == 3:
            in_specs.append(pl.BlockSpec((None, Ts, e.shape[2]), lambda b, s: (b, s, 0)))
        else:
            in_specs.append(pl.BlockSpec(e.shape, lambda b, s: (0, 0)))
        args.append(e)
    in_specs.append(pl.BlockSpec((1, dv), lambda b, s: (0, 0)))
    args.append(gain.reshape(1, dv))
    blocks = (2 * _nbytes((Ts, wk), F32) + 3 * _nbytes((Ts, wv), F32) + _nbytes((Ts, LANES), F32)
              + _nbytes((LANES, wk), BF16))
    scratch = 3 * _nbytes((Ts, wk), F32) + _nbytes((H, dv, dk), F32)
    return pl.pallas_call(
        functools.partial(_linattn_kernel, mode=mode, H=H, dk=dk, dv=dv, C=C, Ts=Ts),
        grid=(B, S // Ts),
        in_specs=in_specs,
        out_specs=pl.BlockSpec((None, Ts, wv), lambda b, s: (b, s, 0)),
        out_shape=jax.ShapeDtypeStruct((B, S, wv), F32),
        scratch_shapes=[
            pltpu.VMEM((H, dv, dk), F32),
            pltpu.VMEM((Ts, wk), F32),
            pltpu.VMEM((Ts, wk), F32),
            pltpu.VMEM((Ts, wk), F32),
        ],
        compiler_params=pltpu.CompilerParams(
            dimension_semantics=("parallel", "arbitrary"),
            vmem_limit_bytes=_vmem_limit(blocks, scratch)),
        name="linattn_" + mode,
    )(*args)


def _fox_gate_kernel(ff_ref, bias_ref, c_ref, *, S, CB):
    r = lax.broadcasted_iota(jnp.int32, (CB, CB), 0)
    c = lax.broadcasted_iota(jnp.int32, (CB, CB), 1)
    tril = jnp.where(r >= c, 1.0, 0.0).astype(F32)
    carry = jnp.zeros((1, LANES), F32)
    for i in range(S // CB):
        rows = slice(i * CB, (i + 1) * CB)
        log_f = _log_sigmoid(ff_ref[rows, :] + bias_ref[...])
        cs = jnp.dot(tril, log_f, precision=lax.Precision.HIGHEST,
                     preferred_element_type=F32) + carry
        c_ref[rows, :] = cs
        carry = cs[CB - 1:CB, :]


def _fox_gate(ff, bias, B, S):
    CB = min(CUMSUM_BLOCK, S)
    assert S % CB == 0
    return pl.pallas_call(
        functools.partial(_fox_gate_kernel, S=S, CB=CB),
        grid=(B,),
        in_specs=[
            pl.BlockSpec((None, S, LANES), lambda b: (b, 0, 0)),
            pl.BlockSpec((1, LANES), lambda b: (0, 0)),
        ],
        out_specs=pl.BlockSpec((None, S, LANES), lambda b: (b, 0, 0)),
        out_shape=jax.ShapeDtypeStruct((B, S, LANES), F32),
        compiler_params=pltpu.CompilerParams(
            dimension_semantics=("parallel",),
            vmem_limit_bytes=_vmem_limit(2 * _nbytes((S, LANES), F32))),
        name="fox_gate",
    )(ff, bias)


def _attn_kernel(*refs, mode, d, BLK, nb, topk, G):
    if mode == "moba":
        q_ref, k_ref, v_ref, slope_ref, o_ref, kb_sc, vt_sc, acc_sc, kmean_sc = refs
    else:
        q_ref, k_ref, v_ref, ccol_ref, crow_ref, o_ref, kb_sc, vt_sc, acc_sc, crep_sc = refs
    hg = pl.program_id(1)
    j = pl.program_id(2)
    S = nb * BLK

    @pl.when(j == 0)
    def _():
        for g in range(G):
            cd = slice(g * d, (g + 1) * d)
            if mode == "moba":
                kmean_sc[g] = jnp.zeros(kmean_sc.shape[1:], F32)
            else:
                lane = lax.broadcasted_iota(jnp.int32, (S, LANES), 1)
                c_key = jnp.sum(jnp.where(lane == hg * G + g, ccol_ref[...], 0.0),
                                axis=-1, keepdims=True)
            for n in range(nb):
                rows = slice(n * BLK, (n + 1) * BLK)
                if mode == "fox":
                    crep_sc[g, n] = jnp.broadcast_to(c_key[rows], (BLK, LANES))
                k_blk = k_ref[rows, cd]
                kb_sc[g, n] = k_blk.astype(BF16)
                vt_sc[g, n] = v_ref[rows, cd].T.astype(BF16)
                if mode == "moba":
                    kmean_sc[g, n:n + 1, :] = jnp.mean(k_blk, axis=0, keepdims=True)

    key_i = lax.broadcasted_iota(jnp.int32, (BLK, BLK), 0)
    qry_i = lax.broadcasted_iota(jnp.int32, (BLK, BLK), 1)
    causal = qry_i >= key_i
    qb, slope, selected = [], [], []
    if mode == "moba":
        rel = (qry_i - key_i).astype(F32)
        blk_i = lax.broadcasted_iota(jnp.int32, (kmean_sc.shape[1], BLK), 0)

    for g in range(G):
        q = q_ref[:, g * d:(g + 1) * d] * (d ** -0.5)
        qb.append(q.astype(BF16))
        if mode == "moba":
            gate = lax.dot_general(kmean_sc[g], q, _NT, precision=lax.Precision.HIGHEST,
                                   preferred_element_type=F32)
            rank = jnp.zeros(gate.shape, F32)
            for m in range(nb):
                gm = gate[m:m + 1, :]
                beats = jnp.where(gm > gate, 1.0,
                                  jnp.where((gm == gate) & (blk_i > m), 1.0, 0.0))
                rank = rank + beats * (m < j).astype(F32)
            selected.append(jnp.where((blk_i < j) & (rank < topk), 1.0, 0.0))
            slope.append(slope_ref[g, :, :1])

    def logits(g, n):
        s = lax.dot_general(kb_sc[g, n], qb[g], _NT, preferred_element_type=F32)
        if mode == "moba":
            return s - slope[g] * (rel + ((j - n) * BLK).astype(F32))
        c_key = crep_sc[g, n]
        c_qry = crow_ref[g, pl.ds(j, 1), :]
        return s + (c_qry - jnp.concatenate([c_key] * (BLK // LANES), axis=1))

    scores = [jnp.where(causal, logits(g, j), MASKED_LOGIT) for g in range(G)]
    stats, probs = [], []
    for g in range(G):
        m0 = jnp.max(scores[g], axis=0, keepdims=True)
        p = jnp.exp(scores[g] - m0)
        stats += [m0, jnp.sum(p, axis=0, keepdims=True)]
        probs.append(p.astype(BF16))
    for g in range(G):
        acc_sc[g] = jnp.dot(vt_sc[g, j], probs[g], preferred_element_type=F32)

    def past_block(n, stats):
        scores = [logits(g, n) for g in range(G)]
        out, probs, alphas = [], [], []
        for g in range(G):
            m_prev, l_prev = stats[2 * g], stats[2 * g + 1]
            s = scores[g]
            if mode == "moba":
                keep = jnp.sum(jnp.where(blk_i == n, selected[g], 0.0), axis=0, keepdims=True)
                s = jnp.where(keep > 0.5, s, MASKED_LOGIT)
            m_new = jnp.maximum(m_prev, jnp.max(s, axis=0, keepdims=True))
            alpha = jnp.exp(m_prev - m_new)
            p = jnp.exp(s - m_new)
            out += [m_new, alpha * l_prev + jnp.sum(p, axis=0, keepdims=True)]
            probs.append(p.astype(BF16))
            alphas.append(alpha)
        updates = [jnp.dot(vt_sc[g, n], probs[g], preferred_element_type=F32) for g in range(G)]
        for g in range(G):
            acc_sc[g] = alphas[g] * acc_sc[g] + updates[g]
        return tuple(out)

    stats = lax.fori_loop(0, j, past_block, tuple(stats))
    for g in range(G):
        o_ref[:, g * d:(g + 1) * d] = (acc_sc[g] / stats[2 * g + 1]).T


def _attention(mode, u, col_blocks, extras, B, S, H, d):
    BLK = min(ATTN_BLOCK, S)
    G = ATTN_HEADS_PER_STEP
    assert S % BLK == 0 and BLK % LANES == 0 and H % G == 0
    nb = S // BLK
    nbp = -(-nb // SUBLANES) * SUBLANES
    qi, ki, vi = (c // G for c in col_blocks)
    assert all(c % G == 0 for c in col_blocks)
    w = G * d
    in_specs = [
        pl.BlockSpec((None, BLK, w), lambda b, h, j: (b, j, qi + h)),
        pl.BlockSpec((None, S, w), lambda b, h, j: (b, 0, ki + h)),
        pl.BlockSpec((None, S, w), lambda b, h, j: (b, 0, vi + h)),
    ]
    scratch = [pltpu.VMEM((G, nb, BLK, d), BF16), pltpu.VMEM((G, nb, d, BLK), BF16),
               pltpu.VMEM((G, d, BLK), F32)]
    scratch_bytes = 2 * _nbytes((G, S, d), BF16) + _nbytes((G, d, BLK), F32)
    if mode == "moba":
        (slopes,) = extras
        in_specs.append(pl.BlockSpec((G, 1, LANES), lambda b, h, j: (h, 0, 0)))
        scratch.append(pltpu.VMEM((G, nbp, d), F32))
        blocks = 2 * _nbytes((S, w), F32) + 2 * _nbytes((BLK, w), F32)
    else:
        c_cols, c_rows = extras
        in_specs.append(pl.BlockSpec((None, S, LANES), lambda b, h, j: (b, 0, 0)))
        in_specs.append(pl.BlockSpec((None, G, nb, BLK), lambda b, h, j: (b, h, 0, 0)))
        scratch.append(pltpu.VMEM((G, nb, BLK, LANES), F32))
        scratch_bytes += _nbytes((G, S, LANES), F32)
        blocks = (2 * _nbytes((S, w), F32) + 2 * _nbytes((BLK, w), F32)
                  + _nbytes((S, LANES), F32) + _nbytes((G, nb, BLK), F32))
    return pl.pallas_call(
        functools.partial(_attn_kernel, mode=mode, d=d, BLK=BLK, nb=nb, topk=MOBA_TOPK, G=G),
        grid=(B, H // G, nb),
        in_specs=in_specs,
        out_specs=pl.BlockSpec((None, BLK, w), lambda b, h, j: (b, j, h)),
        out_shape=jax.ShapeDtypeStruct((B, S, H * d), F32),
        scratch_shapes=scratch,
        compiler_params=pltpu.CompilerParams(
            dimension_semantics=("parallel", "parallel", "arbitrary"),
            vmem_limit_bytes=_vmem_limit(blocks, scratch_bytes)),
        name="attn_" + mode,
    )(u, u, u, *extras)


def _split_in_proj(w_in, lo, hi):
    main = jnp.concatenate([w_in[:, :lo], w_in[:, hi:]], axis=1).astype(BF16)
    side = jnp.pad(w_in[:, lo:hi], ((0, 0), (0, LANES - (hi - lo)))).astype(BF16)
    return main, side


def kernel(x, ev_norm_mix, ev_w_in, ev_gla_gate_w2, ev_gla_gate_b, ev_gla_out_norm, ev_w_out, ev_norm_mlp, ev_w_up, ev_w_down, od_norm_mix, od_w_in, od_fox_fgate_b, od_hgrn_out_norm, od_w_out, od_norm_mlp, od_w_up, od_w_down, hgrn_lb_raw, final_norm):
    B, S, D = x.shape
    T = B * S
    depth = hgrn_lb_raw.shape[0]
    lb_soft = jax.nn.softmax(hgrn_lb_raw.astype(F32), axis=0)
    lb_all = jnp.cumsum(lb_soft, axis=0) - lb_soft[0]
    slopes = jnp.exp2(-8.0 * jnp.arange(1, MOBA_HEADS + 1, dtype=F32) / MOBA_HEADS)
    slopes = jnp.broadcast_to(slopes[:, None, None], (MOBA_HEADS, 1, LANES))

    h = x.reshape(T, D)
    for layer in range(depth):
        e = layer // 2
        last = layer == depth - 1
        if layer % 2 == 0:
            gla_w = GLA_HEADS * GLA_DK
            lo = 2 * gla_w + 2 * GLA_HEADS * GLA_DV
            w_main, w_side = _split_in_proj(ev_w_in[e], lo, lo + GLA_GATE_RANK)
            u, glr = _norm_matmul(h, ev_norm_mix[e], w_main, w_side)
            u = u.reshape(B, S, -1)
            w2 = jnp.pad(ev_gla_gate_w2[e], ((0, LANES - GLA_GATE_RANK), (0, 0))).astype(BF16)
            mix_a = _linattn("gla", u, (0, 1, 1, 2),
                             [glr.reshape(B, S, LANES), w2, ev_gla_gate_b[e].reshape(1, gla_w)],
                             ev_gla_out_norm[e], B, S, GLA_HEADS, GLA_DK, GLA_DV)
            m0 = lo // MOBA_HD
            mix_b = _attention("moba", u, (m0, m0 + MOBA_HEADS, m0 + 2 * MOBA_HEADS), [slopes],
                               B, S, MOBA_HEADS, MOBA_HD)
            w_out, norm_mlp, w_up, w_down = ev_w_out[e], ev_norm_mlp[e], ev_w_up[e], ev_w_down[e]
        else:
            fox_w = FOX_HEADS * FOX_HD
            lo = 3 * fox_w
            w_main, w_side = _split_in_proj(od_w_in[e], lo, lo + FOX_HEADS)
            u, ff = _norm_matmul(h, od_norm_mix[e], w_main, w_side)
            u = u.reshape(B, S, -1)
            bias = jnp.pad(od_fox_fgate_b[e], (0, LANES - FOX_HEADS)).reshape(1, LANES)
            c_cols = _fox_gate(ff.reshape(B, S, LANES), bias, B, S)
            blk = min(ATTN_BLOCK, S)
            c_rows = c_cols[:, :, :FOX_HEADS].transpose(0, 2, 1).reshape(B, FOX_HEADS, S // blk, blk)
            mix_a = _attention("fox", u, (0, FOX_HEADS, 2 * FOX_HEADS), [c_cols, c_rows],
                               B, S, FOX_HEADS, FOX_HD)
            hw = HGRN_HEADS * HGRN_DK
            h0 = lo // hw
            mix_b = _linattn("hgrn", u, (h0, h0 + 1, h0 + 2, h0 + 3),
                             [lb_all[layer].reshape(1, hw)],
                             od_hgrn_out_norm[e], B, S, HGRN_HEADS, HGRN_DK, HGRN_DV)
            w_out, norm_mlp, w_up, w_down = od_w_out[e], od_norm_mlp[e], od_w_up[e], od_w_down[e]
        h = _out_proj(mix_a.reshape(T, -1), mix_b.reshape(T, -1), w_out.astype(BF16), h)
        h = _mlp(h, norm_mlp, w_up.astype(BF16), w_down.astype(BF16), final_norm, last)
    return h.reshape(B, S, D)
```

```python
import functools

import jax
import jax.numpy as jnp
from jax import lax
from jax.experimental import pallas as pl
from jax.experimental.pallas import tpu as pltpu

F32 = jnp.float32
BF16 = jnp.bfloat16
ACT = BF16

EPS = 1e-6
D_MODEL = 2048
D_FF = 4 * D_MODEL
GLA_HEADS, GLA_DK, GLA_DV = 4, 128, 256
GLA_GATE_RANK = 16
GLA_GATE_TAU = 16.0
MOBA_HEADS, MOBA_HD, MOBA_BLOCK, MOBA_TOPK = 8, 128, 256, 3
FOX_HEADS, FOX_HD = 8, 128
HGRN_HEADS, HGRN_DK, HGRN_DV = 8, 128, 128
LINATTN_CHUNK = 64

LANES = 128
SUBLANES = 8
MIB = 1024 * 1024
VMEM_INTERNAL_SCRATCH = 8 * MIB

PROJ_TM, PROJ_TN = 1024, 1024
OUT_TM = 512
MLP_TM, MLP_TF = 1024, 512
LINATTN_TS = 512
ATTN_BLOCK = 256
ATTN_HEADS_PER_STEP = 4
CUMSUM_BLOCK = 256
LINATTN_SAFE_LOG_DECAY = 60.0
MASKED_LOGIT = -1e30

_NT = (((1,), (1,)), ((), ()))
_TN = (((0,), (0,)), ((), ()))


def _vmem_limit(pipelined_bytes, scratch_bytes=0):
    return int(2 * pipelined_bytes + scratch_bytes + VMEM_INTERNAL_SCRATCH)


def _nbytes(shape, dtype):
    n = 1
    for s in shape:
        n *= s
    return n * jnp.dtype(dtype).itemsize


def _sigmoid(x):
    return 1.0 / (1.0 + jnp.exp(-x))


def _log_sigmoid(x):
    return jnp.minimum(x, 0.0) - jnp.log1p(jnp.exp(-jnp.abs(x)))


def _rms_scale(x):
    return x * lax.rsqrt(jnp.mean(x * x, axis=-1, keepdims=True) + EPS)


def _norm_matmul_kernel(x_ref, g_ref, w_ref, ws_ref, o_ref, os_ref, xn_ref):
    @pl.when(pl.program_id(1) == 0)
    def _():
        xn = (_rms_scale(x_ref[...]) * g_ref[...]).astype(BF16)
        xn_ref[...] = xn
        os_ref[...] = jnp.dot(xn, ws_ref[...], preferred_element_type=F32)

    o_ref[...] = jnp.dot(xn_ref[...], w_ref[...], preferred_element_type=F32).astype(o_ref.dtype)


def _norm_matmul(x, g, w, ws):
    T, D = x.shape
    N = w.shape[1]
    tm, tn = min(PROJ_TM, T), PROJ_TN
    assert T % tm == 0 and N % tn == 0 and ws.shape == (D, LANES)
    blocks = (_nbytes((tm, D), F32) + _nbytes((D, tn), BF16) + _nbytes((D, LANES), BF16)
              + _nbytes((tm, tn), ACT) + _nbytes((tm, LANES), F32))
    return pl.pallas_call(
        _norm_matmul_kernel,
        grid=(T // tm, N // tn),
        in_specs=[
            pl.BlockSpec((tm, D), lambda i, j: (i, 0)),
            pl.BlockSpec((1, D), lambda i, j: (0, 0)),
            pl.BlockSpec((D, tn), lambda i, j: (0, j)),
            pl.BlockSpec((D, LANES), lambda i, j: (0, 0)),
        ],
        out_specs=[
            pl.BlockSpec((tm, tn), lambda i, j: (i, j)),
            pl.BlockSpec((tm, LANES), lambda i, j: (i, 0)),
        ],
        out_shape=[jax.ShapeDtypeStruct((T, N), ACT), jax.ShapeDtypeStruct((T, LANES), F32)],
        scratch_shapes=[pltpu.VMEM((tm, D), BF16)],
        compiler_params=pltpu.CompilerParams(
            dimension_semantics=("parallel", "arbitrary"),
            vmem_limit_bytes=_vmem_limit(blocks, _nbytes((tm, D), BF16))),
        name="norm_in_proj",
    )(x, g.reshape(1, D), w, ws)


def _out_proj_kernel(a_ref, b_ref, wa_ref, wb_ref, h_ref, o_ref):
    acc = jnp.dot(a_ref[...].astype(BF16), wa_ref[...], preferred_element_type=F32)
    acc += jnp.dot(b_ref[...].astype(BF16), wb_ref[...], preferred_element_type=F32)
    o_ref[...] = h_ref[...] + acc


def _out_proj(a, b, w, h):
    T, D = h.shape
    Ka, Kb = a.shape[1], b.shape[1]
    tm = min(OUT_TM, T)
    assert T % tm == 0 and Ka == Kb and w.shape == (Ka + Kb, D)
    blocks = (_nbytes((tm, Ka), a.dtype) + _nbytes((tm, Kb), b.dtype) + _nbytes((Ka, D), BF16)
              + _nbytes((Kb, D), BF16) + 2 * _nbytes((tm, D), F32))
    return pl.pallas_call(
        _out_proj_kernel,
        grid=(T // tm,),
        in_specs=[
            pl.BlockSpec((tm, Ka), lambda i: (i, 0)),
            pl.BlockSpec((tm, Kb), lambda i: (i, 0)),
            pl.BlockSpec((Ka, D), lambda i: (0, 0)),
            pl.BlockSpec((Kb, D), lambda i: (1, 0)),
            pl.BlockSpec((tm, D), lambda i: (i, 0)),
        ],
        out_specs=pl.BlockSpec((tm, D), lambda i: (i, 0)),
        out_shape=jax.ShapeDtypeStruct((T, D), F32),
        compiler_params=pltpu.CompilerParams(
            dimension_semantics=("parallel",), vmem_limit_bytes=_vmem_limit(blocks)),
        name="out_proj",
    )(a, b, w, w, h)


def _mlp_kernel(h_ref, g_ref, wu_ref, wd_ref, gf_ref, o_ref, xn_ref, *, final_norm):
    f = pl.program_id(1)

    @pl.when(f == 0)
    def _():
        x = h_ref[...]
        xn_ref[...] = (_rms_scale(x) * g_ref[...]).astype(BF16)
        o_ref[...] = x

    hid = jnp.dot(xn_ref[...], wu_ref[...], preferred_element_type=F32)
    hid = jnp.square(jnp.maximum(hid, 0.0))
    o_ref[...] += jnp.dot(hid.astype(BF16), wd_ref[...], preferred_element_type=F32)

    if final_norm:
        @pl.when(f == pl.num_programs(1) - 1)
        def _():
            o_ref[...] = _rms_scale(o_ref[...]) * gf_ref[...]


def _mlp(h, g, wu, wd, gf, final_norm):
    T, D = h.shape
    FF = wu.shape[1]
    tm, tf = min(MLP_TM, T), MLP_TF
    assert T % tm == 0 and FF % tf == 0
    blocks = (2 * _nbytes((tm, D), F32) + _nbytes((D, tf), BF16) + _nbytes((tf, D), BF16))
    return pl.pallas_call(
        functools.partial(_mlp_kernel, final_norm=final_norm),
        grid=(T // tm, FF // tf),
        in_specs=[
            pl.BlockSpec((tm, D), lambda i, f: (i, 0)),
            pl.BlockSpec((1, D), lambda i, f: (0, 0)),
            pl.BlockSpec((D, tf), lambda i, f: (0, f)),
            pl.BlockSpec((tf, D), lambda i, f: (f, 0)),
            pl.BlockSpec((1, D), lambda i, f: (0, 0)),
        ],
        out_specs=pl.BlockSpec((tm, D), lambda i, f: (i, 0)),
        out_shape=jax.ShapeDtypeStruct((T, D), F32),
        scratch_shapes=[pltpu.VMEM((tm, D), BF16)],
        compiler_params=pltpu.CompilerParams(
            dimension_semantics=("parallel", "arbitrary"),
            vmem_limit_bytes=_vmem_limit(blocks, _nbytes((tm, D), BF16))),
        name="mlp",
    )(h, g.reshape(1, D), wu, wd, gf.reshape(1, D))


def _linattn_kernel(*refs, mode, H, dk, dv, C, Ts):
    if mode == "gla":
        (q_ref, k_ref, v_ref, g_ref, glr_ref, w2_ref, b2_ref, gain_ref,
         o_ref, st_ref, b_sc, q_sc, k_sc) = refs
    else:
        (q_ref, k_ref, v_ref, g_ref, lb_ref, gain_ref,
         o_ref, st_ref, b_sc, q_sc, k_sc) = refs

    @pl.when(pl.program_id(1) == 0)
    def _():
        st_ref[...] = jnp.zeros_like(st_ref)

    n_chunks = Ts // C
    pos = lax.broadcasted_iota(jnp.int32, (C, H * dk), 0)

    def prepare(c, b_min):
        rows = pl.ds(pl.multiple_of(c * C, C), C)
        if mode == "gla":
            z = jnp.dot(glr_ref[rows, :].astype(BF16), w2_ref[...],
                        preferred_element_type=F32) + b2_ref[...]
            b = _log_sigmoid(z) * (1.0 / GLA_GATE_TAU)
            q_sc[rows, :] = q_ref[rows, :].astype(F32) * (dk ** -0.5)
            k_sc[rows, :] = k_ref[rows, :].astype(F32)
        else:
            lb = lb_ref[...]
            f = lb + (1.0 - lb) * _sigmoid(k_ref[rows, :].astype(F32))
            b = jnp.log(f)
            k_sc[rows, :] = 1.0 - f
            x = q_ref[rows, :].astype(F32)
            q_sc[rows, :] = x * _sigmoid(x)
        shift = 1
        while shift < C:
            b = b + jnp.where(pos >= shift, pltpu.roll(b, shift, axis=0), 0.0)
            shift *= 2
        b_sc[rows, :] = b
        return jnp.minimum(b_min, b[C - 1:C, :])

    b_min = lax.fori_loop(0, n_chunks, prepare, jnp.zeros((1, H * dk), F32))
    factorable = jnp.min(b_min) >= -LINATTN_SAFE_LOG_DECAY

    row = lax.broadcasted_iota(jnp.int32, (C, C), 0)
    col = lax.broadcasted_iota(jnp.int32, (C, C), 1)
    gain = gain_ref[...]

    def chunk(c, direct):
        r0 = pl.multiple_of(c * C, C)
        rows = pl.ds(r0, C)
        inter, scores, vs, new_state = [], [], [], []
        for h in range(H):
            ck = slice(h * dk, (h + 1) * dk)
            b = b_sc[rows, ck]
            q = q_sc[rows, ck]
            k = k_sc[rows, ck]
            v = v_ref[rows, h * dv:(h + 1) * dv].astype(BF16)
            b_last = b[C - 1:C, :]
            qe = (q * jnp.exp(b)).astype(BF16)
            state = st_ref[h]
            inter.append(lax.dot_general(qe, state.astype(BF16), _NT, preferred_element_type=F32))
            if direct:
                def columns(s8, sc, ck=ck, b=b, q=q):
                    r8 = pl.ds(pl.multiple_of(r0 + s8 * SUBLANES, SUBLANES), SUBLANES)
                    b8 = b_sc[r8, ck]
                    k8 = k_sc[r8, ck]
                    for i in range(SUBLANES):
                        decay = jnp.exp(b - b8[i:i + 1, :])
                        contrib = jnp.sum(q * (k8[i:i + 1, :] * decay), axis=-1, keepdims=True)
                        sc = jnp.where(col == s8 * SUBLANES + i, contrib, sc)
                    return sc
                sc = lax.fori_loop(0, C // SUBLANES, columns, jnp.zeros((C, C), F32))
            else:
                ke = (k * jnp.exp(-b)).astype(BF16)
                sc = lax.dot_general(qe, ke, _NT, preferred_element_type=F32)
            scores.append(sc)
            k_tail = (k * jnp.exp(b_last - b)).astype(BF16)
            new_state.append(state * jnp.exp(b_last) + lax.dot_general(
                v, k_tail, _TN, preferred_element_type=F32))
            vs.append(v)
        outs = []
        for h in range(H):
            sc = jnp.where(row >= col, scores[h], 0.0).astype(BF16)
            outs.append(inter[h] + jnp.dot(sc, vs[h], preferred_element_type=F32))
        for h in range(H):
            cv = slice(h * dv, (h + 1) * dv)
            st_ref[h] = new_state[h]
            g = g_ref[rows, cv].astype(F32)
            o_ref[rows, cv] = ((_rms_scale(outs[h]) * gain) * (g * _sigmoid(g))).astype(o_ref.dtype)

    @pl.when(factorable)
    def _():
        lax.fori_loop(0, n_chunks, lambda c, _: (chunk(c, False), 0)[1], 0)

    @pl.when(jnp.logical_not(factorable))
    def _():
        lax.fori_loop(0, n_chunks, lambda c, _: (chunk(c, True), 0)[1], 0)


def _linattn(mode, u, col_blocks, extras, gain, B, S, H, dk, dv):
    C = LINATTN_CHUNK
    Ts = min(LINATTN_TS, S)
    assert S % Ts == 0 and Ts % C == 0
    wk, wv = H * dk, H * dv
    qi, ki, vi, gi = col_blocks
    in_specs = [
        pl.BlockSpec((None, Ts, wk), lambda b, s: (b, s, qi)),
        pl.BlockSpec((None, Ts, wk), lambda b, s: (b, s, ki)),
        pl.BlockSpec((None, Ts, wv), lambda b, s: (b, s, vi)),
        pl.BlockSpec((None, Ts, wv), lambda b, s: (b, s, gi)),
    ]
    args = [u, u, u, u]
    for e in extras:
        if e.ndim == 3:
            in_specs.append(pl.BlockSpec((None, Ts, e.shape[2]), lambda b, s: (b, s, 0)))
        else:
            in_specs.append(pl.BlockSpec(e.shape, lambda b, s: (0, 0)))
        args.append(e)
    in_specs.append(pl.BlockSpec((1, dv), lambda b, s: (0, 0)))
    args.append(gain.reshape(1, dv))
    blocks = (2 * _nbytes((Ts, wk), ACT) + 3 * _nbytes((Ts, wv), ACT) + _nbytes((Ts, LANES), F32)
              + _nbytes((LANES, wk), BF16))
    scratch = 3 * _nbytes((Ts, wk), F32) + _nbytes((H, dv, dk), F32)
    return pl.pallas_call(
        functools.partial(_linattn_kernel, mode=mode, H=H, dk=dk, dv=dv, C=C, Ts=Ts),
        grid=(B, S // Ts),
        in_specs=in_specs,
        out_specs=pl.BlockSpec((None, Ts, wv), lambda b, s: (b, s, 0)),
        out_shape=jax.ShapeDtypeStruct((B, S, wv), ACT),
        scratch_shapes=[
            pltpu.VMEM((H, dv, dk), F32),
            pltpu.VMEM((Ts, wk), F32),
            pltpu.VMEM((Ts, wk), F32),
            pltpu.VMEM((Ts, wk), F32),
        ],
        compiler_params=pltpu.CompilerParams(
            dimension_semantics=("parallel", "arbitrary"),
            vmem_limit_bytes=_vmem_limit(blocks, scratch)),
        name="linattn_" + mode,
    )(*args)


def _fox_gate_kernel(ff_ref, bias_ref, c_ref, *, S, CB):
    r = lax.broadcasted_iota(jnp.int32, (CB, CB), 0)
    c = lax.broadcasted_iota(jnp.int32, (CB, CB), 1)
    tril = jnp.where(r >= c, 1.0, 0.0).astype(F32)
    carry = jnp.zeros((1, LANES), F32)
    for i in range(S // CB):
        rows = slice(i * CB, (i + 1) * CB)
        log_f = _log_sigmoid(ff_ref[rows, :] + bias_ref[...])
        cs = jnp.dot(tril, log_f, precision=lax.Precision.HIGHEST,
                     preferred_element_type=F32) + carry
        c_ref[rows, :] = cs
        carry = cs[CB - 1:CB, :]


def _fox_gate(ff, bias, B, S):
    CB = min(CUMSUM_BLOCK, S)
    assert S % CB == 0
    return pl.pallas_call(
        functools.partial(_fox_gate_kernel, S=S, CB=CB),
        grid=(B,),
        in_specs=[
            pl.BlockSpec((None, S, LANES), lambda b: (b, 0, 0)),
            pl.BlockSpec((1, LANES), lambda b: (0, 0)),
        ],
        out_specs=pl.BlockSpec((None, S, LANES), lambda b: (b, 0, 0)),
        out_shape=jax.ShapeDtypeStruct((B, S, LANES), F32),
        compiler_params=pltpu.CompilerParams(
            dimension_semantics=("parallel",),
            vmem_limit_bytes=_vmem_limit(2 * _nbytes((S, LANES), F32))),
        name="fox_gate",
    )(ff, bias)


def _attn_kernel(*refs, mode, d, BLK, nb, topk, G):
    if mode == "moba":
        q_ref, k_ref, v_ref, slope_ref, o_ref, vt_sc, acc_sc, kmean_sc = refs
    else:
        q_ref, k_ref, v_ref, ccol_ref, crow_ref, o_ref, vt_sc, acc_sc, crep_sc = refs
    assert q_ref.dtype == BF16 and k_ref.dtype == BF16 and v_ref.dtype == BF16
    hg = pl.program_id(1)
    j = pl.program_id(2)
    S = nb * BLK

    @pl.when(j == 0)
    def _():
        for g in range(G):
            cd = slice(g * d, (g + 1) * d)
            if mode == "moba":
                kmean_sc[g] = jnp.zeros(kmean_sc.shape[1:], F32)
            else:
                lane = lax.broadcasted_iota(jnp.int32, (S, LANES), 1)
                c_key = jnp.sum(jnp.where(lane == hg * G + g, ccol_ref[...], 0.0),
                                axis=-1, keepdims=True)
            for n in range(nb):
                rows = slice(n * BLK, (n + 1) * BLK)
                if mode == "fox":
                    crep_sc[g, n] = jnp.broadcast_to(c_key[rows], (BLK, LANES))
                vt_sc[g, n] = v_ref[rows, cd].astype(F32).T.astype(BF16)
                if mode == "moba":
                    kmean_sc[g, n:n + 1, :] = jnp.mean(k_ref[rows, cd].astype(F32),
                                                       axis=0, keepdims=True)

    key_i = lax.broadcasted_iota(jnp.int32, (BLK, BLK), 0)
    qry_i = lax.broadcasted_iota(jnp.int32, (BLK, BLK), 1)
    causal = qry_i >= key_i
    qb, slope, selected = [], [], []
    if mode == "moba":
        rel = (qry_i - key_i).astype(F32)
        blk_i = lax.broadcasted_iota(jnp.int32, (kmean_sc.shape[1], BLK), 0)

    for g in range(G):
        q_raw = q_ref[:, g * d:(g + 1) * d]
        qb.append((q_raw.astype(F32) * (d ** -0.5)).astype(BF16))
        if mode == "moba":
            kmean = kmean_sc[g]
            hi = kmean.astype(BF16).astype(F32)
            mid = (kmean - hi).astype(BF16).astype(F32)
            lo = (kmean - hi) - mid
            nbp = kmean.shape[0]
            parts = lax.dot_general(jnp.concatenate([hi, mid, lo], axis=0).astype(BF16), q_raw,
                                    _NT, preferred_element_type=F32)
            gate = (parts[:nbp] + parts[nbp:2 * nbp] + parts[2 * nbp:]) * (d ** -0.5)
            rank = jnp.zeros(gate.shape, F32)
            for m in range(nb):
                gm = gate[m:m + 1, :]
                beats = jnp.where(gm > gate, 1.0,
                                  jnp.where((gm == gate) & (blk_i > m), 1.0, 0.0))
                rank = rank + beats * (m < j).astype(F32)
            selected.append(jnp.where((blk_i < j) & (rank < topk), 1.0, 0.0))
            slope.append(slope_ref[g, :, :1])

    def logits(g, n):
        k_blk = k_ref[pl.ds(pl.multiple_of(n * BLK, BLK), BLK), g * d:(g + 1) * d]
        s = lax.dot_general(k_blk, qb[g], _NT, preferred_element_type=F32)
        if mode == "moba":
            return s - slope[g] * (rel + ((j - n) * BLK).astype(F32))
        c_key = crep_sc[g, n]
        c_qry = crow_ref[g, pl.ds(j, 1), :]
        return s + (c_qry - jnp.concatenate([c_key] * (BLK // LANES), axis=1))

    scores = [jnp.where(causal, logits(g, j), MASKED_LOGIT) for g in range(G)]
    stats, probs = [], []
    for g in range(G):
        m0 = jnp.max(scores[g], axis=0, keepdims=True)
        p = jnp.exp(scores[g] - m0)
        stats += [m0, jnp.sum(p, axis=0, keepdims=True)]
        probs.append(p.astype(BF16))
    for g in range(G):
        acc_sc[g] = jnp.dot(vt_sc[g, j], probs[g], preferred_element_type=F32)

    def past_block(n, stats):
        scores = [logits(g, n) for g in range(G)]
        out, probs, alphas = [], [], []
        for g in range(G):
            m_prev, l_prev = stats[2 * g], stats[2 * g + 1]
            s = scores[g]
            if mode == "moba":
                keep = jnp.sum(jnp.where(blk_i == n, selected[g], 0.0), axis=0, keepdims=True)
                s = jnp.where(keep > 0.5, s, MASKED_LOGIT)
            m_new = jnp.maximum(m_prev, jnp.max(s, axis=0, keepdims=True))
            alpha = jnp.exp(m_prev - m_new)
            p = jnp.exp(s - m_new)
            out += [m_new, alpha * l_prev + jnp.sum(p, axis=0, keepdims=True)]
            probs.append(p.astype(BF16))
            alphas.append(alpha)
        updates = [jnp.dot(vt_sc[g, n], probs[g], preferred_element_type=F32) for g in range(G)]
        for g in range(G):
            acc_sc[g] = alphas[g] * acc_sc[g] + updates[g]
        return tuple(out)

    stats = lax.fori_loop(0, j, past_block, tuple(stats))
    for g in range(G):
        o_ref[:, g * d:(g + 1) * d] = (acc_sc[g] / stats[2 * g + 1]).T.astype(o_ref.dtype)


def _attention(mode, u, col_blocks, extras, B, S, H, d):
    BLK = min(ATTN_BLOCK, S)
    G = ATTN_HEADS_PER_STEP
    assert S % BLK == 0 and BLK % LANES == 0 and H % G == 0
    nb = S // BLK
    nbp = -(-nb // SUBLANES) * SUBLANES
    qi, ki, vi = (c // G for c in col_blocks)
    assert all(c % G == 0 for c in col_blocks)
    w = G * d
    in_specs = [
        pl.BlockSpec((None, BLK, w), lambda b, h, j: (b, j, qi + h)),
        pl.BlockSpec((None, S, w), lambda b, h, j: (b, 0, ki + h)),
        pl.BlockSpec((None, S, w), lambda b, h, j: (b, 0, vi + h)),
    ]
    scratch = [pltpu.VMEM((G, nb, d, BLK), BF16), pltpu.VMEM((G, d, BLK), F32)]
    scratch_bytes = _nbytes((G, S, d), BF16) + _nbytes((G, d, BLK), F32)
    if mode == "moba":
        (slopes,) = extras
        in_specs.append(pl.BlockSpec((G, 1, LANES), lambda b, h, j: (h, 0, 0)))
        scratch.append(pltpu.VMEM((G, nbp, d), F32))
        blocks = 2 * _nbytes((S, w), ACT) + 2 * _nbytes((BLK, w), ACT)
    else:
        c_cols, c_rows = extras
        in_specs.append(pl.BlockSpec((None, S, LANES), lambda b, h, j: (b, 0, 0)))
        in_specs.append(pl.BlockSpec((None, G, nb, BLK), lambda b, h, j: (b, h, 0, 0)))
        scratch.append(pltpu.VMEM((G, nb, BLK, LANES), F32))
        scratch_bytes += _nbytes((G, S, LANES), F32)
        blocks = (2 * _nbytes((S, w), ACT) + 2 * _nbytes((BLK, w), ACT)
                  + _nbytes((S, LANES), F32) + _nbytes((G, nb, BLK), F32))
    return pl.pallas_call(
        functools.partial(_attn_kernel, mode=mode, d=d, BLK=BLK, nb=nb, topk=MOBA_TOPK, G=G),
        grid=(B, H // G, nb),
        in_specs=in_specs,
        out_specs=pl.BlockSpec((None, BLK, w), lambda b, h, j: (b, j, h)),
        out_shape=jax.ShapeDtypeStruct((B, S, H * d), ACT),
        scratch_shapes=scratch,
        compiler_params=pltpu.CompilerParams(
            dimension_semantics=("parallel", "parallel", "arbitrary"),
            vmem_limit_bytes=_vmem_limit(blocks, scratch_bytes)),
        name="attn_" + mode,
    )(u, u, u, *extras)


def _split_in_proj(w_in, lo, hi):
    main = jnp.concatenate([w_in[:, :lo], w_in[:, hi:]], axis=1).astype(BF16)
    side = jnp.pad(w_in[:, lo:hi], ((0, 0), (0, LANES - (hi - lo)))).astype(BF16)
    return main, side


def kernel(x, ev_norm_mix, ev_w_in, ev_gla_gate_w2, ev_gla_gate_b, ev_gla_out_norm, ev_w_out, ev_norm_mlp, ev_w_up, ev_w_down, od_norm_mix, od_w_in, od_fox_fgate_b, od_hgrn_out_norm, od_w_out, od_norm_mlp, od_w_up, od_w_down, hgrn_lb_raw, final_norm):
    B, S, D = x.shape
    T = B * S
    depth = hgrn_lb_raw.shape[0]
    lb_soft = jax.nn.softmax(hgrn_lb_raw.astype(F32), axis=0)
    lb_all = jnp.cumsum(lb_soft, axis=0) - lb_soft[0]
    slopes = jnp.exp2(-8.0 * jnp.arange(1, MOBA_HEADS + 1, dtype=F32) / MOBA_HEADS)
    slopes = jnp.broadcast_to(slopes[:, None, None], (MOBA_HEADS, 1, LANES))

    h = x.reshape(T, D)
    for layer in range(depth):
        e = layer // 2
        last = layer == depth - 1
        if layer % 2 == 0:
            gla_w = GLA_HEADS * GLA_DK
            lo = 2 * gla_w + 2 * GLA_HEADS * GLA_DV
            w_main, w_side = _split_in_proj(ev_w_in[e], lo, lo + GLA_GATE_RANK)
            u, glr = _norm_matmul(h, ev_norm_mix[e], w_main, w_side)
            u = u.reshape(B, S, -1)
            w2 = jnp.pad(ev_gla_gate_w2[e], ((0, LANES - GLA_GATE_RANK), (0, 0))).astype(BF16)
            mix_a = _linattn("gla", u, (0, 1, 1, 2),
                             [glr.reshape(B, S, LANES), w2, ev_gla_gate_b[e].reshape(1, gla_w)],
                             ev_gla_out_norm[e], B, S, GLA_HEADS, GLA_DK, GLA_DV)
            m0 = lo // MOBA_HD
            mix_b = _attention("moba", u, (m0, m0 + MOBA_HEADS, m0 + 2 * MOBA_HEADS), [slopes],
                               B, S, MOBA_HEADS, MOBA_HD)
            w_out, norm_mlp, w_up, w_down = ev_w_out[e], ev_norm_mlp[e], ev_w_up[e], ev_w_down[e]
        else:
            fox_w = FOX_HEADS * FOX_HD
            lo = 3 * fox_w
            w_main, w_side = _split_in_proj(od_w_in[e], lo, lo + FOX_HEADS)
            u, ff = _norm_matmul(h, od_norm_mix[e], w_main, w_side)
            u = u.reshape(B, S, -1)
            bias = jnp.pad(od_fox_fgate_b[e], (0, LANES - FOX_HEADS)).reshape(1, LANES)
            c_cols = _fox_gate(ff.reshape(B, S, LANES), bias, B, S)
            blk = min(ATTN_BLOCK, S)
            c_rows = c_cols[:, :, :FOX_HEADS].transpose(0, 2, 1).reshape(B, FOX_HEADS, S // blk, blk)
            mix_a = _attention("fox", u, (0, FOX_HEADS, 2 * FOX_HEADS), [c_cols, c_rows],
                               B, S, FOX_HEADS, FOX_HD)
            hw = HGRN_HEADS * HGRN_DK
            h0 = lo // hw
            mix_b = _linattn("hgrn", u, (h0, h0 + 1, h0 + 2, h0 + 3),
                             [lb_all[layer].reshape(1, hw)],
                             od_hgrn_out_norm[e], B, S, HGRN_HEADS, HGRN_DK, HGRN_DV)
            w_out, norm_mlp, w_up, w_down = od_w_out[e], od_norm_mlp[e], od_w_up[e], od_w_down[e]
        h = _out_proj(mix_a.reshape(T, -1), mix_b.reshape(T, -1), w_out.astype(BF16), h)
        h = _mlp(h, norm_mlp, w_up.astype(BF16), w_down.astype(BF16), final_norm, last)
    return h.reshape(B, S, D)
```

```python
import functools

import jax
import jax.numpy as jnp
from jax import lax
from jax.experimental import pallas as pl
from jax.experimental.pallas import tpu as pltpu

F32 = jnp.float32
BF16 = jnp.bfloat16
ACT = BF16

EPS = 1e-6
D_MODEL = 2048
D_FF = 4 * D_MODEL
GLA_HEADS, GLA_DK, GLA_DV = 4, 128, 256
GLA_GATE_RANK = 16
GLA_GATE_TAU = 16.0
MOBA_HEADS, MOBA_HD, MOBA_BLOCK, MOBA_TOPK = 8, 128, 256, 3
FOX_HEADS, FOX_HD = 8, 128
HGRN_HEADS, HGRN_DK, HGRN_DV = 8, 128, 128
LINATTN_CHUNK = 64

LANES = 128
SUBLANES = 8
MIB = 1024 * 1024
VMEM_INTERNAL_SCRATCH = 8 * MIB

PROJ_TM, PROJ_TN = 1024, 1024
OUT_TM = 512
MLP_TM, MLP_TF = 1024, 512
LINATTN_TS = 512
ATTN_BLOCK = 256
ATTN_HEADS_PER_STEP = 8
CUMSUM_BLOCK = 256
LINATTN_SAFE_LOG_DECAY = 60.0
MASKED_LOGIT = -1e30
LOG2E = 1.4426950408889634

_NT = (((1,), (1,)), ((), ()))
_TN = (((0,), (0,)), ((), ()))


def _vmem_limit(pipelined_bytes, scratch_bytes=0):
    return int(2 * pipelined_bytes + scratch_bytes + VMEM_INTERNAL_SCRATCH)


def _nbytes(shape, dtype):
    n = 1
    for s in shape:
        n *= s
    return n * jnp.dtype(dtype).itemsize


def _sigmoid(x):
    return 1.0 / (1.0 + jnp.exp(-x))


def _log_sigmoid(x):
    return jnp.minimum(x, 0.0) - jnp.log1p(jnp.exp(-jnp.abs(x)))


def _rms_scale(x):
    return x * lax.rsqrt(jnp.mean(x * x, axis=-1, keepdims=True) + EPS)


def _norm_matmul_kernel(x_ref, g_ref, w_ref, ws_ref, o_ref, os_ref, xn_ref):
    @pl.when(pl.program_id(1) == 0)
    def _():
        xn = (_rms_scale(x_ref[...]) * g_ref[...]).astype(BF16)
        xn_ref[...] = xn
        os_ref[...] = jnp.dot(xn, ws_ref[...], preferred_element_type=F32)

    o_ref[...] = jnp.dot(xn_ref[...], w_ref[...], preferred_element_type=F32).astype(o_ref.dtype)


def _norm_matmul(x, g, w, ws):
    T, D = x.shape
    N = w.shape[1]
    tm, tn = min(PROJ_TM, T), PROJ_TN
    assert T % tm == 0 and N % tn == 0 and ws.shape == (D, LANES)
    blocks = (_nbytes((tm, D), F32) + _nbytes((D, tn), BF16) + _nbytes((D, LANES), BF16)
              + _nbytes((tm, tn), ACT) + _nbytes((tm, LANES), F32))
    return pl.pallas_call(
        _norm_matmul_kernel,
        grid=(T // tm, N // tn),
        in_specs=[
            pl.BlockSpec((tm, D), lambda i, j: (i, 0)),
            pl.BlockSpec((1, D), lambda i, j: (0, 0)),
            pl.BlockSpec((D, tn), lambda i, j: (0, j)),
            pl.BlockSpec((D, LANES), lambda i, j: (0, 0)),
        ],
        out_specs=[
            pl.BlockSpec((tm, tn), lambda i, j: (i, j)),
            pl.BlockSpec((tm, LANES), lambda i, j: (i, 0)),
        ],
        out_shape=[jax.ShapeDtypeStruct((T, N), ACT), jax.ShapeDtypeStruct((T, LANES), F32)],
        scratch_shapes=[pltpu.VMEM((tm, D), BF16)],
        compiler_params=pltpu.CompilerParams(
            dimension_semantics=("parallel", "arbitrary"),
            vmem_limit_bytes=_vmem_limit(blocks, _nbytes((tm, D), BF16))),
        name="norm_in_proj",
    )(x, g.reshape(1, D), w, ws)


def _out_proj_kernel(a_ref, b_ref, wa_ref, wb_ref, h_ref, o_ref):
    acc = jnp.dot(a_ref[...].astype(BF16), wa_ref[...], preferred_element_type=F32)
    acc += jnp.dot(b_ref[...].astype(BF16), wb_ref[...], preferred_element_type=F32)
    o_ref[...] = h_ref[...] + acc


def _out_proj(a, b, w, h):
    T, D = h.shape
    Ka, Kb = a.shape[1], b.shape[1]
    tm = min(OUT_TM, T)
    assert T % tm == 0 and Ka == Kb and w.shape == (Ka + Kb, D)
    blocks = (_nbytes((tm, Ka), a.dtype) + _nbytes((tm, Kb), b.dtype) + _nbytes((Ka, D), BF16)
              + _nbytes((Kb, D), BF16) + 2 * _nbytes((tm, D), F32))
    return pl.pallas_call(
        _out_proj_kernel,
        grid=(T // tm,),
        in_specs=[
            pl.BlockSpec((tm, Ka), lambda i: (i, 0)),
            pl.BlockSpec((tm, Kb), lambda i: (i, 0)),
            pl.BlockSpec((Ka, D), lambda i: (0, 0)),
            pl.BlockSpec((Kb, D), lambda i: (1, 0)),
            pl.BlockSpec((tm, D), lambda i: (i, 0)),
        ],
        out_specs=pl.BlockSpec((tm, D), lambda i: (i, 0)),
        out_shape=jax.ShapeDtypeStruct((T, D), F32),
        compiler_params=pltpu.CompilerParams(
            dimension_semantics=("parallel",), vmem_limit_bytes=_vmem_limit(blocks)),
        name="out_proj",
    )(a, b, w, w, h)


def _mlp_kernel(h_ref, g_ref, wu_ref, wd_ref, gf_ref, o_ref, xn_ref, *, final_norm):
    f = pl.program_id(1)

    @pl.when(f == 0)
    def _():
        x = h_ref[...]
        xn_ref[...] = (_rms_scale(x) * g_ref[...]).astype(BF16)
        o_ref[...] = x

    hid = jnp.dot(xn_ref[...], wu_ref[...], preferred_element_type=F32)
    hid = jnp.square(jnp.maximum(hid, 0.0))
    o_ref[...] += jnp.dot(hid.astype(BF16), wd_ref[...], preferred_element_type=F32)

    if final_norm:
        @pl.when(f == pl.num_programs(1) - 1)
        def _():
            o_ref[...] = _rms_scale(o_ref[...]) * gf_ref[...]


def _mlp(h, g, wu, wd, gf, final_norm):
    T, D = h.shape
    FF = wu.shape[1]
    tm, tf = min(MLP_TM, T), MLP_TF
    assert T % tm == 0 and FF % tf == 0
    blocks = (2 * _nbytes((tm, D), F32) + _nbytes((D, tf), BF16) + _nbytes((tf, D), BF16))
    return pl.pallas_call(
        functools.partial(_mlp_kernel, final_norm=final_norm),
        grid=(T // tm, FF // tf),
        in_specs=[
            pl.BlockSpec((tm, D), lambda i, f: (i, 0)),
            pl.BlockSpec((1, D), lambda i, f: (0, 0)),
            pl.BlockSpec((D, tf), lambda i, f: (0, f)),
            pl.BlockSpec((tf, D), lambda i, f: (f, 0)),
            pl.BlockSpec((1, D), lambda i, f: (0, 0)),
        ],
        out_specs=pl.BlockSpec((tm, D), lambda i, f: (i, 0)),
        out_shape=jax.ShapeDtypeStruct((T, D), F32),
        scratch_shapes=[pltpu.VMEM((tm, D), BF16)],
        compiler_params=pltpu.CompilerParams(
            dimension_semantics=("parallel", "arbitrary"),
            vmem_limit_bytes=_vmem_limit(blocks, _nbytes((tm, D), BF16))),
        name="mlp",
    )(h, g.reshape(1, D), wu, wd, gf.reshape(1, D))


def _linattn_kernel(*refs, mode, H, dk, dv, C, Ts):
    if mode == "gla":
        (q_ref, k_ref, v_ref, g_ref, glr_ref, w2_ref, b2_ref, gain_ref,
         o_ref, st_ref, b_sc, q_sc, k_sc) = refs
    else:
        (q_ref, k_ref, v_ref, g_ref, lb_ref, gain_ref,
         o_ref, st_ref, b_sc, q_sc, k_sc) = refs

    @pl.when(pl.program_id(1) == 0)
    def _():
        st_ref[...] = jnp.zeros_like(st_ref)

    n_chunks = Ts // C
    pos = lax.broadcasted_iota(jnp.int32, (C, H * dk), 0)

    def prepare(c, b_min):
        rows = pl.ds(pl.multiple_of(c * C, C), C)
        if mode == "gla":
            z = jnp.dot(glr_ref[rows, :].astype(BF16), w2_ref[...],
                        preferred_element_type=F32) + b2_ref[...]
            b = _log_sigmoid(z) * (1.0 / GLA_GATE_TAU)
            q_sc[rows, :] = q_ref[rows, :].astype(F32) * (dk ** -0.5)
            k_sc[rows, :] = k_ref[rows, :].astype(F32)
        else:
            lb = lb_ref[...]
            f = lb + (1.0 - lb) * _sigmoid(k_ref[rows, :].astype(F32))
            b = jnp.log(f)
            k_sc[rows, :] = 1.0 - f
            x = q_ref[rows, :].astype(F32)
            q_sc[rows, :] = x * _sigmoid(x)
        shift = 1
        while shift < C:
            b = b + jnp.where(pos >= shift, pltpu.roll(b, shift, axis=0), 0.0)
            shift *= 2
        b_sc[rows, :] = b
        return jnp.minimum(b_min, b[C - 1:C, :])

    b_min = lax.fori_loop(0, n_chunks, prepare, jnp.zeros((1, H * dk), F32))
    factorable = jnp.min(b_min) >= -LINATTN_SAFE_LOG_DECAY

    row = lax.broadcasted_iota(jnp.int32, (C, C), 0)
    col = lax.broadcasted_iota(jnp.int32, (C, C), 1)
    gain = gain_ref[...]

    def chunk(c, direct):
        r0 = pl.multiple_of(c * C, C)
        rows = pl.ds(r0, C)
        inter, scores, vs, new_state = [], [], [], []
        for h in range(H):
            ck = slice(h * dk, (h + 1) * dk)
            b = b_sc[rows, ck]
            q = q_sc[rows, ck]
            k = k_sc[rows, ck]
            v = v_ref[rows, h * dv:(h + 1) * dv].astype(BF16)
            b_last = b[C - 1:C, :]
            qe = (q * jnp.exp(b)).astype(BF16)
            state = st_ref[h]
            inter.append(lax.dot_general(qe, state.astype(BF16), _NT, preferred_element_type=F32))
            if direct:
                def columns(s8, sc, ck=ck, b=b, q=q):
                    r8 = pl.ds(pl.multiple_of(r0 + s8 * SUBLANES, SUBLANES), SUBLANES)
                    b8 = b_sc[r8, ck]
                    k8 = k_sc[r8, ck]
                    for i in range(SUBLANES):
                        decay = jnp.exp(b - b8[i:i + 1, :])
                        contrib = jnp.sum(q * (k8[i:i + 1, :] * decay), axis=-1, keepdims=True)
                        sc = jnp.where(col == s8 * SUBLANES + i, contrib, sc)
                    return sc
                sc = lax.fori_loop(0, C // SUBLANES, columns, jnp.zeros((C, C), F32))
            else:
                ke = (k * jnp.exp(-b)).astype(BF16)
                sc = lax.dot_general(qe, ke, _NT, preferred_element_type=F32)
            scores.append(sc)
            k_tail = (k * jnp.exp(b_last - b)).astype(BF16)
            new_state.append(state * jnp.exp(b_last) + lax.dot_general(
                v, k_tail, _TN, preferred_element_type=F32))
            vs.append(v)
        outs = []
        for h in range(H):
            sc = jnp.where(row >= col, scores[h], 0.0).astype(BF16)
            outs.append(inter[h] + jnp.dot(sc, vs[h], preferred_element_type=F32))
        for h in range(H):
            cv = slice(h * dv, (h + 1) * dv)
            st_ref[h] = new_state[h]
            g = g_ref[rows, cv].astype(F32)
            o_ref[rows, cv] = ((_rms_scale(outs[h]) * gain) * (g * _sigmoid(g))).astype(o_ref.dtype)

    @pl.when(factorable)
    def _():
        lax.fori_loop(0, n_chunks, lambda c, _: (chunk(c, False), 0)[1], 0)

    @pl.when(jnp.logical_not(factorable))
    def _():
        lax.fori_loop(0, n_chunks, lambda c, _: (chunk(c, True), 0)[1], 0)


def _linattn(mode, u, col_blocks, extras, gain, B, S, H, dk, dv):
    C = LINATTN_CHUNK
    Ts = min(LINATTN_TS, S)
    assert S % Ts == 0 and Ts % C == 0
    wk, wv = H * dk, H * dv
    qi, ki, vi, gi = col_blocks
    in_specs = [
        pl.BlockSpec((None, Ts, wk), lambda b, s: (b, s, qi)),
        pl.BlockSpec((None, Ts, wk), lambda b, s: (b, s, ki)),
        pl.BlockSpec((None, Ts, wv), lambda b, s: (b, s, vi)),
        pl.BlockSpec((None, Ts, wv), lambda b, s: (b, s, gi)),
    ]
    args = [u, u, u, u]
    for e in extras:
        if e.ndim == 3:
            in_specs.append(pl.BlockSpec((None, Ts, e.shape[2]), lambda b, s: (b, s, 0)))
        else:
            in_specs.append(pl.BlockSpec(e.shape, lambda b, s: (0, 0)))
        args.append(e)
    in_specs.append(pl.BlockSpec((1, dv), lambda b, s: (0, 0)))
    args.append(gain.reshape(1, dv))
    blocks = (2 * _nbytes((Ts, wk), ACT) + 3 * _nbytes((Ts, wv), ACT) + _nbytes((Ts, LANES), F32)
              + _nbytes((LANES, wk), BF16))
    scratch = 3 * _nbytes((Ts, wk), F32) + _nbytes((H, dv, dk), F32)
    return pl.pallas_call(
        functools.partial(_linattn_kernel, mode=mode, H=H, dk=dk, dv=dv, C=C, Ts=Ts),
        grid=(B, S // Ts),
        in_specs=in_specs,
        out_specs=pl.BlockSpec((None, Ts, wv), lambda b, s: (b, s, 0)),
        out_shape=jax.ShapeDtypeStruct((B, S, wv), ACT),
        scratch_shapes=[
            pltpu.VMEM((H, dv, dk), F32),
            pltpu.VMEM((Ts, wk), F32),
            pltpu.VMEM((Ts, wk), F32),
            pltpu.VMEM((Ts, wk), F32),
        ],
        compiler_params=pltpu.CompilerParams(
            dimension_semantics=("parallel", "arbitrary"),
            vmem_limit_bytes=_vmem_limit(blocks, scratch)),
        name="linattn_" + mode,
    )(*args)


def _fox_gate_kernel(ff_ref, bias_ref, c_ref, *, S, CB):
    r = lax.broadcasted_iota(jnp.int32, (CB, CB), 0)
    c = lax.broadcasted_iota(jnp.int32, (CB, CB), 1)
    tril = jnp.where(r >= c, 1.0, 0.0).astype(F32)
    carry = jnp.zeros((1, LANES), F32)
    for i in range(S // CB):
        rows = slice(i * CB, (i + 1) * CB)
        log_f = _log_sigmoid(ff_ref[rows, :] + bias_ref[...])
        cs = jnp.dot(tril, log_f, precision=lax.Precision.HIGHEST,
                     preferred_element_type=F32) + carry
        c_ref[rows, :] = cs
        carry = cs[CB - 1:CB, :]


def _fox_gate(ff, bias, B, S):
    CB = min(CUMSUM_BLOCK, S)
    assert S % CB == 0
    return pl.pallas_call(
        functools.partial(_fox_gate_kernel, S=S, CB=CB),
        grid=(B,),
        in_specs=[
            pl.BlockSpec((None, S, LANES), lambda b: (b, 0, 0)),
            pl.BlockSpec((1, LANES), lambda b: (0, 0)),
        ],
        out_specs=pl.BlockSpec((None, S, LANES), lambda b: (b, 0, 0)),
        out_shape=jax.ShapeDtypeStruct((B, S, LANES), F32),
        compiler_params=pltpu.CompilerParams(
            dimension_semantics=("parallel",),
            vmem_limit_bytes=_vmem_limit(2 * _nbytes((S, LANES), F32))),
        name="fox_gate",
    )(ff, bias)


def _attn_kernel(*refs, mode, d, BLK, nb, topk, G):
    if mode == "moba":
        q_ref, k_ref, v_ref, slope_ref, o_ref, vt_sc, acc_sc, kmean_sc = refs
    else:
        q_ref, k_ref, v_ref, ccol_ref, crow_ref, o_ref, vt_sc, acc_sc, crep_sc = refs
    assert q_ref.dtype == BF16 and k_ref.dtype == BF16 and v_ref.dtype == BF16
    hg = pl.program_id(1)
    j = pl.program_id(2)
    S = nb * BLK

    @pl.when(j == 0)
    def _():
        for g in range(G):
            cd = slice(g * d, (g + 1) * d)
            if mode == "moba":
                kmean_sc[g] = jnp.zeros(kmean_sc.shape[1:], F32)
            else:
                lane = lax.broadcasted_iota(jnp.int32, (S, LANES), 1)
                c_key = jnp.sum(jnp.where(lane == hg * G + g, ccol_ref[...], 0.0),
                                axis=-1, keepdims=True)
            for n in range(nb):
                rows = slice(n * BLK, (n + 1) * BLK)
                if mode == "fox":
                    crep_sc[g, n] = jnp.broadcast_to(c_key[rows] * LOG2E, (BLK, LANES))
                vt_sc[g, n] = v_ref[rows, cd].astype(F32).T.astype(BF16)
                if mode == "moba":
                    kmean_sc[g, n:n + 1, :] = jnp.mean(k_ref[rows, cd].astype(F32),
                                                       axis=0, keepdims=True)

    key_i = lax.broadcasted_iota(jnp.int32, (BLK, BLK), 0)
    qry_i = lax.broadcasted_iota(jnp.int32, (BLK, BLK), 1)
    causal = qry_i >= key_i
    qb, slope, selected = [], [], []
    if mode == "moba":
        rel = (qry_i - key_i).astype(F32)
        blk_i = lax.broadcasted_iota(jnp.int32, (kmean_sc.shape[1], BLK), 0)

    for g in range(G):
        q_raw = q_ref[:, g * d:(g + 1) * d]
        qb.append((q_raw.astype(F32) * (d ** -0.5 * LOG2E)).astype(BF16))
        if mode == "moba":
            kmean = kmean_sc[g]
            hi = kmean.astype(BF16).astype(F32)
            mid = (kmean - hi).astype(BF16).astype(F32)
            lo = (kmean - hi) - mid
            nbp = kmean.shape[0]
            parts = lax.dot_general(jnp.concatenate([hi, mid, lo], axis=0).astype(BF16), q_raw,
                                    _NT, preferred_element_type=F32)
            gate = (parts[:nbp] + parts[nbp:2 * nbp] + parts[2 * nbp:]) * (d ** -0.5)
            rank = jnp.zeros(gate.shape, F32)
            for m in range(nb):
                gm = gate[m:m + 1, :]
                beats = jnp.where(gm > gate, 1.0,
                                  jnp.where((gm == gate) & (blk_i > m), 1.0, 0.0))
                rank = rank + beats * (m < j).astype(F32)
            selected.append(jnp.where((blk_i < j) & (rank < topk), 1.0, 0.0))
            slope.append(slope_ref[g, :, :1] * LOG2E)

    def logits(g, n):
        k_blk = k_ref[pl.ds(pl.multiple_of(n * BLK, BLK), BLK), g * d:(g + 1) * d]
        s = lax.dot_general(k_blk, qb[g], _NT, preferred_element_type=F32)
        if mode == "moba":
            return s - slope[g] * (rel + ((j - n) * BLK).astype(F32))
        c_key = crep_sc[g, n]
        c_qry = crow_ref[g, pl.ds(j, 1), :] * LOG2E
        return s + (c_qry - jnp.concatenate([c_key] * (BLK // LANES), axis=1))

    scores = [jnp.where(causal, logits(g, j), MASKED_LOGIT) for g in range(G)]
    stats, probs = [], []
    for g in range(G):
        m0 = jnp.max(scores[g], axis=0, keepdims=True)
        p = jnp.exp2(scores[g] - m0)
        stats += [m0, jnp.sum(p, axis=0, keepdims=True)]
        probs.append(p.astype(BF16))
    for g in range(G):
        acc_sc[g] = jnp.dot(vt_sc[g, j], probs[g], preferred_element_type=F32)

    def past_block(n, stats):
        scores = [logits(g, n) for g in range(G)]
        out, probs, alphas = [], [], []
        for g in range(G):
            m_prev, l_prev = stats[2 * g], stats[2 * g + 1]
            s = scores[g]
            if mode == "moba":
                keep = jnp.sum(jnp.where(blk_i == n, selected[g], 0.0), axis=0, keepdims=True)
                s = jnp.where(keep > 0.5, s, MASKED_LOGIT)
            m_new = jnp.maximum(m_prev, jnp.max(s, axis=0, keepdims=True))
            alpha = jnp.exp2(m_prev - m_new)
            p = jnp.exp2(s - m_new)
            out += [m_new, alpha * l_prev + jnp.sum(p, axis=0, keepdims=True)]
            probs.append(p.astype(BF16))
            alphas.append(alpha)
        updates = [jnp.dot(vt_sc[g, n], probs[g], preferred_element_type=F32) for g in range(G)]
        for g in range(G):
            acc_sc[g] = alphas[g] * acc_sc[g] + updates[g]
        return tuple(out)

    stats = lax.fori_loop(0, j, past_block, tuple(stats))
    for g in range(G):
        o_ref[:, g * d:(g + 1) * d] = (acc_sc[g] / stats[2 * g + 1]).T.astype(o_ref.dtype)


def _attention(mode, u, col_blocks, extras, B, S, H, d):
    BLK = min(ATTN_BLOCK, S)
    G = ATTN_HEADS_PER_STEP
    assert S % BLK == 0 and BLK % LANES == 0 and H % G == 0
    nb = S // BLK
    nbp = -(-nb // SUBLANES) * SUBLANES
    qi, ki, vi = (c // G for c in col_blocks)
    assert all(c % G == 0 for c in col_blocks)
    w = G * d
    in_specs = [
        pl.BlockSpec((None, BLK, w), lambda b, h, j: (b, j, qi + h)),
        pl.BlockSpec((None, S, w), lambda b, h, j: (b, 0, ki + h)),
        pl.BlockSpec((None, S, w), lambda b, h, j: (b, 0, vi + h)),
    ]
    scratch = [pltpu.VMEM((G, nb, d, BLK), BF16), pltpu.VMEM((G, d, BLK), F32)]
    scratch_bytes = _nbytes((G, S, d), BF16) + _nbytes((G, d, BLK), F32)
    if mode == "moba":
        (slopes,) = extras
        in_specs.append(pl.BlockSpec((G, 1, LANES), lambda b, h, j: (h, 0, 0)))
        scratch.append(pltpu.VMEM((G, nbp, d), F32))
        blocks = 2 * _nbytes((S, w), ACT) + 2 * _nbytes((BLK, w), ACT)
    else:
        c_cols, c_rows = extras
        in_specs.append(pl.BlockSpec((None, S, LANES), lambda b, h, j: (b, 0, 0)))
        in_specs.append(pl.BlockSpec((None, G, nb, BLK), lambda b, h, j: (b, h, 0, 0)))
        scratch.append(pltpu.VMEM((G, nb, BLK, LANES), F32))
        scratch_bytes += _nbytes((G, S, LANES), F32)
        blocks = (2 * _nbytes((S, w), ACT) + 2 * _nbytes((BLK, w), ACT)
                  + _nbytes((S, LANES), F32) + _nbytes((G, nb, BLK), F32))
    return pl.pallas_call(
        functools.partial(_attn_kernel, mode=mode, d=d, BLK=BLK, nb=nb, topk=MOBA_TOPK, G=G),
        grid=(B, H // G, nb),
        in_specs=in_specs,
        out_specs=pl.BlockSpec((None, BLK, w), lambda b, h, j: (b, j, h)),
        out_shape=jax.ShapeDtypeStruct((B, S, H * d), ACT),
        scratch_shapes=scratch,
        compiler_params=pltpu.CompilerParams(
            dimension_semantics=("parallel", "parallel", "arbitrary"),
            vmem_limit_bytes=_vmem_limit(blocks, scratch_bytes)),
        name="attn_" + mode,
    )(u, u, u, *extras)


def _split_in_proj(w_in, lo, hi):
    main = jnp.concatenate([w_in[:, :lo], w_in[:, hi:]], axis=1).astype(BF16)
    side = jnp.pad(w_in[:, lo:hi], ((0, 0), (0, LANES - (hi - lo)))).astype(BF16)
    return main, side


def kernel(x, ev_norm_mix, ev_w_in, ev_gla_gate_w2, ev_gla_gate_b, ev_gla_out_norm, ev_w_out, ev_norm_mlp, ev_w_up, ev_w_down, od_norm_mix, od_w_in, od_fox_fgate_b, od_hgrn_out_norm, od_w_out, od_norm_mlp, od_w_up, od_w_down, hgrn_lb_raw, final_norm):
    B, S, D = x.shape
    T = B * S
    depth = hgrn_lb_raw.shape[0]
    lb_soft = jax.nn.softmax(hgrn_lb_raw.astype(F32), axis=0)
    lb_all = jnp.cumsum(lb_soft, axis=0) - lb_soft[0]
    slopes = jnp.exp2(-8.0 * jnp.arange(1, MOBA_HEADS + 1, dtype=F32) / MOBA_HEADS)
    slopes = jnp.broadcast_to(slopes[:, None, None], (MOBA_HEADS, 1, LANES))

    h = x.reshape(T, D)
    for layer in range(depth):
        e = layer // 2
        last = layer == depth - 1
        if layer % 2 == 0:
            gla_w = GLA_HEADS * GLA_DK
            lo = 2 * gla_w + 2 * GLA_HEADS * GLA_DV
            w_main, w_side = _split_in_proj(ev_w_in[e], lo, lo + GLA_GATE_RANK)
            u, glr = _norm_matmul(h, ev_norm_mix[e], w_main, w_side)
            u = u.reshape(B, S, -1)
            w2 = jnp.pad(ev_gla_gate_w2[e], ((0, LANES - GLA_GATE_RANK), (0, 0))).astype(BF16)
            mix_a = _linattn("gla", u, (0, 1, 1, 2),
                             [glr.reshape(B, S, LANES), w2, ev_gla_gate_b[e].reshape(1, gla_w)],
                             ev_gla_out_norm[e], B, S, GLA_HEADS, GLA_DK, GLA_DV)
            m0 = lo // MOBA_HD
            mix_b = _attention("moba", u, (m0, m0 + MOBA_HEADS, m0 + 2 * MOBA_HEADS), [slopes],
                               B, S, MOBA_HEADS, MOBA_HD)
            w_out, norm_mlp, w_up, w_down = ev_w_out[e], ev_norm_mlp[e], ev_w_up[e], ev_w_down[e]
        else:
            fox_w = FOX_HEADS * FOX_HD
            lo = 3 * fox_w
            w_main, w_side = _split_in_proj(od_w_in[e], lo, lo + FOX_HEADS)
            u, ff = _norm_matmul(h, od_norm_mix[e], w_main, w_side)
            u = u.reshape(B, S, -1)
            bias = jnp.pad(od_fox_fgate_b[e], (0, LANES - FOX_HEADS)).reshape(1, LANES)
            c_cols = _fox_gate(ff.reshape(B, S, LANES), bias, B, S)
            blk = min(ATTN_BLOCK, S)
            c_rows = c_cols[:, :, :FOX_HEADS].transpose(0, 2, 1).reshape(B, FOX_HEADS, S // blk, blk)
            mix_a = _attention("fox", u, (0, FOX_HEADS, 2 * FOX_HEADS), [c_cols, c_rows],
                               B, S, FOX_HEADS, FOX_HD)
            hw = HGRN_HEADS * HGRN_DK
            h0 = lo // hw
            mix_b = _linattn("hgrn", u, (h0, h0 + 1, h0 + 2, h0 + 3),
                             [lb_all[layer].reshape(1, hw)],
                             od_hgrn_out_norm[e], B, S, HGRN_HEADS, HGRN_DK, HGRN_DV)
            w_out, norm_mlp, w_up, w_down = od_w_out[e], od_norm_mlp[e], od_w_up[e], od_w_down[e]
        h = _out_proj(mix_a.reshape(T, -1), mix_b.reshape(T, -1), w_out.astype(BF16), h)
        h = _mlp(h, norm_mlp, w_up.astype(BF16), w_down.astype(BF16), final_norm, last)
    return h.reshape(B, S, D)
```

```python
import functools

import jax
import jax.numpy as jnp
from jax import lax
from jax.experimental import pallas as pl
from jax.experimental.pallas import tpu as pltpu

F32 = jnp.float32
BF16 = jnp.bfloat16
ACT = BF16

EPS = 1e-6
D_MODEL = 2048
D_FF = 4 * D_MODEL
GLA_HEADS, GLA_DK, GLA_DV = 4, 128, 256
GLA_GATE_RANK = 16
GLA_GATE_TAU = 16.0
MOBA_HEADS, MOBA_HD, MOBA_BLOCK, MOBA_TOPK = 8, 128, 256, 3
FOX_HEADS, FOX_HD = 8, 128
HGRN_HEADS, HGRN_DK, HGRN_DV = 8, 128, 128
LINATTN_CHUNK = 64

LANES = 128
SUBLANES = 8
MIB = 1024 * 1024
VMEM_INTERNAL_SCRATCH = 8 * MIB

PROJ_TM, PROJ_TN = 1024, 1024
OUT_TM = 512
MLP_TM, MLP_TF = 1024, 512
LINATTN_TS = 512
ATTN_BLOCK = 256
ATTN_HEADS_PER_STEP = 8
CUMSUM_BLOCK = 256
LINATTN_SAFE_LOG_DECAY = 60.0
MASKED_LOGIT = -1e30
LOG2E = 1.4426950408889634

_NT = (((1,), (1,)), ((), ()))
_TN = (((0,), (0,)), ((), ()))


def _vmem_limit(pipelined_bytes, scratch_bytes=0):
    return int(2 * pipelined_bytes + scratch_bytes + VMEM_INTERNAL_SCRATCH)


def _nbytes(shape, dtype):
    n = 1
    for s in shape:
        n *= s
    return n * jnp.dtype(dtype).itemsize


def _sigmoid(x):
    return 1.0 / (1.0 + jnp.exp(-x))


def _log_sigmoid(x):
    return jnp.minimum(x, 0.0) - jnp.log1p(jnp.exp(-jnp.abs(x)))


def _rms_scale(x):
    return x * lax.rsqrt(jnp.mean(x * x, axis=-1, keepdims=True) + EPS)


def _norm_matmul_kernel(x_ref, g_ref, w_ref, ws_ref, o_ref, os_ref, xn_ref):
    @pl.when(pl.program_id(1) == 0)
    def _():
        xn = (_rms_scale(x_ref[...]) * g_ref[...]).astype(BF16)
        xn_ref[...] = xn
        os_ref[...] = jnp.dot(xn, ws_ref[...], preferred_element_type=F32)

    o_ref[...] = jnp.dot(xn_ref[...], w_ref[...], preferred_element_type=F32).astype(o_ref.dtype)


def _norm_matmul(x, g, w, ws):
    T, D = x.shape
    N = w.shape[1]
    tm, tn = min(PROJ_TM, T), PROJ_TN
    assert T % tm == 0 and N % tn == 0 and ws.shape == (D, LANES)
    blocks = (_nbytes((tm, D), F32) + _nbytes((D, tn), BF16) + _nbytes((D, LANES), BF16)
              + _nbytes((tm, tn), ACT) + _nbytes((tm, LANES), F32))
    return pl.pallas_call(
        _norm_matmul_kernel,
        grid=(T // tm, N // tn),
        in_specs=[
            pl.BlockSpec((tm, D), lambda i, j: (i, 0)),
            pl.BlockSpec((1, D), lambda i, j: (0, 0)),
            pl.BlockSpec((D, tn), lambda i, j: (0, j)),
            pl.BlockSpec((D, LANES), lambda i, j: (0, 0)),
        ],
        out_specs=[
            pl.BlockSpec((tm, tn), lambda i, j: (i, j)),
            pl.BlockSpec((tm, LANES), lambda i, j: (i, 0)),
        ],
        out_shape=[jax.ShapeDtypeStruct((T, N), ACT), jax.ShapeDtypeStruct((T, LANES), F32)],
        scratch_shapes=[pltpu.VMEM((tm, D), BF16)],
        compiler_params=pltpu.CompilerParams(
            dimension_semantics=("parallel", "arbitrary"),
            vmem_limit_bytes=_vmem_limit(blocks, _nbytes((tm, D), BF16))),
        name="norm_in_proj",
    )(x, g.reshape(1, D), w, ws)


def _out_proj_kernel(a_ref, b_ref, wa_ref, wb_ref, h_ref, o_ref):
    acc = jnp.dot(a_ref[...].astype(BF16), wa_ref[...], preferred_element_type=F32)
    acc += jnp.dot(b_ref[...].astype(BF16), wb_ref[...], preferred_element_type=F32)
    o_ref[...] = h_ref[...] + acc


def _out_proj(a, b, w, h):
    T, D = h.shape
    Ka, Kb = a.shape[1], b.shape[1]
    tm = min(OUT_TM, T)
    assert T % tm == 0 and Ka == Kb and w.shape == (Ka + Kb, D)
    blocks = (_nbytes((tm, Ka), a.dtype) + _nbytes((tm, Kb), b.dtype) + _nbytes((Ka, D), BF16)
              + _nbytes((Kb, D), BF16) + 2 * _nbytes((tm, D), F32))
    return pl.pallas_call(
        _out_proj_kernel,
        grid=(T // tm,),
        in_specs=[
            pl.BlockSpec((tm, Ka), lambda i: (i, 0)),
            pl.BlockSpec((tm, Kb), lambda i: (i, 0)),
            pl.BlockSpec((Ka, D), lambda i: (0, 0)),
            pl.BlockSpec((Kb, D), lambda i: (1, 0)),
            pl.BlockSpec((tm, D), lambda i: (i, 0)),
        ],
        out_specs=pl.BlockSpec((tm, D), lambda i: (i, 0)),
        out_shape=jax.ShapeDtypeStruct((T, D), F32),
        compiler_params=pltpu.CompilerParams(
            dimension_semantics=("parallel",), vmem_limit_bytes=_vmem_limit(blocks)),
        name="out_proj",
    )(a, b, w, w, h)


def _mlp_kernel(h_ref, g_ref, wu_ref, wd_ref, gf_ref, o_ref, xn_ref, *, final_norm):
    f = pl.program_id(1)

    @pl.when(f == 0)
    def _():
        x = h_ref[...]
        xn_ref[...] = (_rms_scale(x) * g_ref[...]).astype(BF16)
        o_ref[...] = x

    hid = jnp.dot(xn_ref[...], wu_ref[...], preferred_element_type=F32)
    hid = jnp.square(jnp.maximum(hid, 0.0))
    o_ref[...] += jnp.dot(hid.astype(BF16), wd_ref[...], preferred_element_type=F32)

    if final_norm:
        @pl.when(f == pl.num_programs(1) - 1)
        def _():
            o_ref[...] = _rms_scale(o_ref[...]) * gf_ref[...]


def _mlp(h, g, wu, wd, gf, final_norm):
    T, D = h.shape
    FF = wu.shape[1]
    tm, tf = min(MLP_TM, T), MLP_TF
    assert T % tm == 0 and FF % tf == 0
    blocks = (2 * _nbytes((tm, D), F32) + _nbytes((D, tf), BF16) + _nbytes((tf, D), BF16))
    return pl.pallas_call(
        functools.partial(_mlp_kernel, final_norm=final_norm),
        grid=(T // tm, FF // tf),
        in_specs=[
            pl.BlockSpec((tm, D), lambda i, f: (i, 0)),
            pl.BlockSpec((1, D), lambda i, f: (0, 0)),
            pl.BlockSpec((D, tf), lambda i, f: (0, f)),
            pl.BlockSpec((tf, D), lambda i, f: (f, 0)),
            pl.BlockSpec((1, D), lambda i, f: (0, 0)),
        ],
        out_specs=pl.BlockSpec((tm, D), lambda i, f: (i, 0)),
        out_shape=jax.ShapeDtypeStruct((T, D), F32),
        scratch_shapes=[pltpu.VMEM((tm, D), BF16)],
        compiler_params=pltpu.CompilerParams(
            dimension_semantics=("parallel", "arbitrary"),
            vmem_limit_bytes=_vmem_limit(blocks, _nbytes((tm, D), BF16))),
        name="mlp",
    )(h, g.reshape(1, D), wu, wd, gf.reshape(1, D))


def _linattn_kernel(*refs, mode, H, dk, dv, C, Ts):
    if mode == "gla":
        (q_ref, k_ref, v_ref, g_ref, glr_ref, w2_ref, b2_ref, gain_ref,
         o_ref, st_ref, b_sc, q_sc, k_sc) = refs
    else:
        (q_ref, k_ref, v_ref, g_ref, lb_ref, gain_ref,
         o_ref, st_ref, b_sc, q_sc, k_sc) = refs

    @pl.when(pl.program_id(1) == 0)
    def _():
        st_ref[...] = jnp.zeros_like(st_ref)

    n_chunks = Ts // C
    pos = lax.broadcasted_iota(jnp.int32, (C, H * dk), 0)

    def prepare(c, b_min):
        rows = pl.ds(pl.multiple_of(c * C, C), C)
        if mode == "gla":
            z = jnp.dot(glr_ref[rows, :].astype(BF16), w2_ref[...],
                        preferred_element_type=F32) + b2_ref[...]
            b = _log_sigmoid(z) * (1.0 / GLA_GATE_TAU)
            q_sc[rows, :] = q_ref[rows, :].astype(F32) * (dk ** -0.5)
            k_sc[rows, :] = k_ref[rows, :].astype(F32)
        else:
            lb = lb_ref[...]
            f = lb + (1.0 - lb) * _sigmoid(k_ref[rows, :].astype(F32))
            b = jnp.log(f)
            k_sc[rows, :] = 1.0 - f
            x = q_ref[rows, :].astype(F32)
            q_sc[rows, :] = x * _sigmoid(x)
        shift = 1
        while shift < C:
            b = b + jnp.where(pos >= shift, pltpu.roll(b, shift, axis=0), 0.0)
            shift *= 2
        b_sc[rows, :] = b
        return jnp.minimum(b_min, b[C - 1:C, :])

    b_min = lax.fori_loop(0, n_chunks, prepare, jnp.zeros((1, H * dk), F32))
    factorable = jnp.min(b_min) >= -LINATTN_SAFE_LOG_DECAY

    row = lax.broadcasted_iota(jnp.int32, (C, C), 0)
    col = lax.broadcasted_iota(jnp.int32, (C, C), 1)
    gain = gain_ref[...]

    def chunk(c, direct):
        r0 = pl.multiple_of(c * C, C)
        rows = pl.ds(r0, C)
        inter, scores, vs, new_state = [], [], [], []
        for h in range(H):
            ck = slice(h * dk, (h + 1) * dk)
            b = b_sc[rows, ck]
            q = q_sc[rows, ck]
            k = k_sc[rows, ck]
            v = v_ref[rows, h * dv:(h + 1) * dv].astype(BF16)
            b_last = b[C - 1:C, :]
            qe = (q * jnp.exp(b)).astype(BF16)
            state = st_ref[h]
            inter.append(lax.dot_general(qe, state.astype(BF16), _NT, preferred_element_type=F32))
            if direct:
                def columns(s8, sc, ck=ck, b=b, q=q):
                    r8 = pl.ds(pl.multiple_of(r0 + s8 * SUBLANES, SUBLANES), SUBLANES)
                    b8 = b_sc[r8, ck]
                    k8 = k_sc[r8, ck]
                    for i in range(SUBLANES):
                        decay = jnp.exp(b - b8[i:i + 1, :])
                        contrib = jnp.sum(q * (k8[i:i + 1, :] * decay), axis=-1, keepdims=True)
                        sc = jnp.where(col == s8 * SUBLANES + i, contrib, sc)
                    return sc
                sc = lax.fori_loop(0, C // SUBLANES, columns, jnp.zeros((C, C), F32))
            else:
                ke = (k * jnp.exp(-b)).astype(BF16)
                sc = lax.dot_general(qe, ke, _NT, preferred_element_type=F32)
            scores.append(sc)
            k_tail = (k * jnp.exp(b_last - b)).astype(BF16)
            new_state.append(state * jnp.exp(b_last) + lax.dot_general(
                v, k_tail, _TN, preferred_element_type=F32))
            vs.append(v)
        outs = []
        for h in range(H):
            sc = jnp.where(row >= col, scores[h], 0.0).astype(BF16)
            outs.append(inter[h] + jnp.dot(sc, vs[h], preferred_element_type=F32))
        for h in range(H):
            cv = slice(h * dv, (h + 1) * dv)
            st_ref[h] = new_state[h]
            g = g_ref[rows, cv].astype(F32)
            o_ref[rows, cv] = ((_rms_scale(outs[h]) * gain) * (g * _sigmoid(g))).astype(o_ref.dtype)

    @pl.when(factorable)
    def _():
        lax.fori_loop(0, n_chunks, lambda c, _: (chunk(c, False), 0)[1], 0)

    @pl.when(jnp.logical_not(factorable))
    def _():
        lax.fori_loop(0, n_chunks, lambda c, _: (chunk(c, True), 0)[1], 0)


def _linattn(mode, u, col_blocks, extras, gain, B, S, H, dk, dv):
    C = LINATTN_CHUNK
    Ts = min(LINATTN_TS, S)
    assert S % Ts == 0 and Ts % C == 0
    wk, wv = H * dk, H * dv
    qi, ki, vi, gi = col_blocks
    in_specs = [
        pl.BlockSpec((None, Ts, wk), lambda b, s: (b, s, qi)),
        pl.BlockSpec((None, Ts, wk), lambda b, s: (b, s, ki)),
        pl.BlockSpec((None, Ts, wv), lambda b, s: (b, s, vi)),
        pl.BlockSpec((None, Ts, wv), lambda b, s: (b, s, gi)),
    ]
    args = [u, u, u, u]
    for e in extras:
        if e.ndim == 3:
            in_specs.append(pl.BlockSpec((None, Ts, e.shape[2]), lambda b, s: (b, s, 0)))
        else:
            in_specs.append(pl.BlockSpec(e.shape, lambda b, s: (0, 0)))
        args.append(e)
    in_specs.append(pl.BlockSpec((1, dv), lambda b, s: (0, 0)))
    args.append(gain.reshape(1, dv))
    blocks = (2 * _nbytes((Ts, wk), ACT) + 3 * _nbytes((Ts, wv), ACT) + _nbytes((Ts, LANES), F32)
              + _nbytes((LANES, wk), BF16))
    scratch = 3 * _nbytes((Ts, wk), F32) + _nbytes((H, dv, dk), F32)
    return pl.pallas_call(
        functools.partial(_linattn_kernel, mode=mode, H=H, dk=dk, dv=dv, C=C, Ts=Ts),
        grid=(B, S // Ts),
        in_specs=in_specs,
        out_specs=pl.BlockSpec((None, Ts, wv), lambda b, s: (b, s, 0)),
        out_shape=jax.ShapeDtypeStruct((B, S, wv), ACT),
        scratch_shapes=[
            pltpu.VMEM((H, dv, dk), F32),
            pltpu.VMEM((Ts, wk), F32),
            pltpu.VMEM((Ts, wk), F32),
            pltpu.VMEM((Ts, wk), F32),
        ],
        compiler_params=pltpu.CompilerParams(
            dimension_semantics=("parallel", "arbitrary"),
            vmem_limit_bytes=_vmem_limit(blocks, scratch)),
        name="linattn_" + mode,
    )(*args)


def _fox_gate_kernel(ff_ref, bias_ref, c_ref, *, S, CB):
    r = lax.broadcasted_iota(jnp.int32, (CB, CB), 0)
    c = lax.broadcasted_iota(jnp.int32, (CB, CB), 1)
    tril = jnp.where(r >= c, 1.0, 0.0).astype(F32)
    carry = jnp.zeros((1, LANES), F32)
    for i in range(S // CB):
        rows = slice(i * CB, (i + 1) * CB)
        log_f = _log_sigmoid(ff_ref[rows, :] + bias_ref[...])
        cs = jnp.dot(tril, log_f, precision=lax.Precision.HIGHEST,
                     preferred_element_type=F32) + carry
        c_ref[rows, :] = cs
        carry = cs[CB - 1:CB, :]


def _fox_gate(ff, bias, B, S):
    CB = min(CUMSUM_BLOCK, S)
    assert S % CB == 0
    return pl.pallas_call(
        functools.partial(_fox_gate_kernel, S=S, CB=CB),
        grid=(B,),
        in_specs=[
            pl.BlockSpec((None, S, LANES), lambda b: (b, 0, 0)),
            pl.BlockSpec((1, LANES), lambda b: (0, 0)),
        ],
        out_specs=pl.BlockSpec((None, S, LANES), lambda b: (b, 0, 0)),
        out_shape=jax.ShapeDtypeStruct((B, S, LANES), F32),
        compiler_params=pltpu.CompilerParams(
            dimension_semantics=("parallel",),
            vmem_limit_bytes=_vmem_limit(2 * _nbytes((S, LANES), F32))),
        name="fox_gate",
    )(ff, bias)


def _attn_kernel(*refs, mode, d, BLK, nb, topk, G):
    if mode == "moba":
        q_ref, k_ref, v_ref, slope_ref, o_ref, vt_sc, acc_sc, kmean_sc, relb_sc = refs
    else:
        q_ref, k_ref, v_ref, ccol_ref, crow_ref, o_ref, vt_sc, acc_sc, crep_sc = refs
    assert q_ref.dtype == BF16 and k_ref.dtype == BF16 and v_ref.dtype == BF16
    hg = pl.program_id(1)
    j = pl.program_id(2)
    S = nb * BLK
    key_i = lax.broadcasted_iota(jnp.int32, (BLK, BLK), 0)
    qry_i = lax.broadcasted_iota(jnp.int32, (BLK, BLK), 1)

    @pl.when(j == 0)
    def _():
        for g in range(G):
            cd = slice(g * d, (g + 1) * d)
            if mode == "moba":
                kmean_sc[g] = jnp.zeros(kmean_sc.shape[1:], F32)
                relb_sc[g] = (slope_ref[g, :, :1] * LOG2E) * (qry_i - key_i).astype(F32)
            else:
                lane = lax.broadcasted_iota(jnp.int32, (S, LANES), 1)
                c_key = jnp.sum(jnp.where(lane == hg * G + g, ccol_ref[...], 0.0),
                                axis=-1, keepdims=True)
            for n in range(nb):
                rows = slice(n * BLK, (n + 1) * BLK)
                if mode == "fox":
                    crep_sc[g, n] = jnp.broadcast_to(c_key[rows] * LOG2E, (BLK, LANES))
                vt_sc[g, n] = v_ref[rows, cd].astype(F32).T.astype(BF16)
                if mode == "moba":
                    kmean_sc[g, n:n + 1, :] = jnp.mean(k_ref[rows, cd].astype(F32),
                                                       axis=0, keepdims=True)

    causal = qry_i >= key_i
    qb, slope, selected = [], [], []
    if mode == "moba":
        blk_i = lax.broadcasted_iota(jnp.int32, (kmean_sc.shape[1], BLK), 0)

    for g in range(G):
        q_raw = q_ref[:, g * d:(g + 1) * d]
        qb.append((q_raw.astype(F32) * (d ** -0.5 * LOG2E)).astype(BF16))
        if mode == "moba":
            kmean = kmean_sc[g]
            hi = kmean.astype(BF16).astype(F32)
            mid = (kmean - hi).astype(BF16).astype(F32)
            lo = (kmean - hi) - mid
            nbp = kmean.shape[0]
            parts = lax.dot_general(jnp.concatenate([hi, mid, lo], axis=0).astype(BF16), q_raw,
                                    _NT, preferred_element_type=F32)
            gate = (parts[:nbp] + parts[nbp:2 * nbp] + parts[2 * nbp:]) * (d ** -0.5)
            rank = jnp.zeros(gate.shape, F32)
            for m in range(nb):
                gm = gate[m:m + 1, :]
                beats = jnp.where(gm > gate, 1.0,
                                  jnp.where((gm == gate) & (blk_i > m), 1.0, 0.0))
                rank = rank + beats * (m < j).astype(F32)
            selected.append(jnp.where((blk_i < j) & (rank < topk), 1.0, 0.0))
            slope.append(slope_ref[g, :, :1] * LOG2E)

    def partial(g, n):
        k_blk = k_ref[pl.ds(pl.multiple_of(n * BLK, BLK), BLK), g * d:(g + 1) * d]
        s = lax.dot_general(k_blk, qb[g], _NT, preferred_element_type=F32)
        if mode == "moba":
            return s - relb_sc[g]
        return s - jnp.concatenate([crep_sc[g, n]] * (BLK // LANES), axis=1)

    def adjust(g, n):
        if mode == "moba":
            return -(slope[g] * ((j - n) * BLK).astype(F32))
        return crow_ref[g, pl.ds(j, 1), :] * LOG2E

    scores = [jnp.where(causal, partial(g, j), MASKED_LOGIT) for g in range(G)]
    stats, probs = [], []
    for g in range(G):
        m0 = jnp.max(scores[g], axis=0, keepdims=True)
        p = jnp.exp2(scores[g] - m0)
        stats += [m0 + adjust(g, j), jnp.sum(p, axis=0, keepdims=True)]
        probs.append(p.astype(BF16))
    for g in range(G):
        acc_sc[g] = jnp.dot(vt_sc[g, j], probs[g], preferred_element_type=F32)

    def past_block(n, stats):
        scores = [partial(g, n) for g in range(G)]
        out, probs, alphas = [], [], []
        for g in range(G):
            m_prev, l_prev = stats[2 * g], stats[2 * g + 1]
            s = scores[g]
            if mode == "moba":
                keep = jnp.sum(jnp.where(blk_i == n, selected[g], 0.0), axis=0, keepdims=True)
                s = jnp.where(keep > 0.5, s, MASKED_LOGIT)
            adj = adjust(g, n)
            m_new = jnp.maximum(m_prev, jnp.max(s, axis=0, keepdims=True) + adj)
            alpha = jnp.exp2(m_prev - m_new)
            p = jnp.exp2(s - (m_new - adj))
            out += [m_new, alpha * l_prev + jnp.sum(p, axis=0, keepdims=True)]
            probs.append(p.astype(BF16))
            alphas.append(alpha)
        updates = [jnp.dot(vt_sc[g, n], probs[g], preferred_element_type=F32) for g in range(G)]
        for g in range(G):
            acc_sc[g] = alphas[g] * acc_sc[g] + updates[g]
        return tuple(out)

    stats = lax.fori_loop(0, j, past_block, tuple(stats))
    for g in range(G):
        o_ref[:, g * d:(g + 1) * d] = (acc_sc[g] / stats[2 * g + 1]).T.astype(o_ref.dtype)


def _attention(mode, u, col_blocks, extras, B, S, H, d):
    BLK = min(ATTN_BLOCK, S)
    G = ATTN_HEADS_PER_STEP
    assert S % BLK == 0 and BLK % LANES == 0 and H % G == 0
    nb = S // BLK
    nbp = -(-nb // SUBLANES) * SUBLANES
    qi, ki, vi = (c // G for c in col_blocks)
    assert all(c % G == 0 for c in col_blocks)
    w = G * d
    in_specs = [
        pl.BlockSpec((None, BLK, w), lambda b, h, j: (b, j, qi + h)),
        pl.BlockSpec((None, S, w), lambda b, h, j: (b, 0, ki + h)),
        pl.BlockSpec((None, S, w), lambda b, h, j: (b, 0, vi + h)),
    ]
    scratch = [pltpu.VMEM((G, nb, d, BLK), BF16), pltpu.VMEM((G, d, BLK), F32)]
    scratch_bytes = _nbytes((G, S, d), BF16) + _nbytes((G, d, BLK), F32)
    if mode == "moba":
        (slopes,) = extras
        in_specs.append(pl.BlockSpec((G, 1, LANES), lambda b, h, j: (h, 0, 0)))
        scratch += [pltpu.VMEM((G, nbp, d), F32), pltpu.VMEM((G, BLK, BLK), F32)]
        scratch_bytes += _nbytes((G, BLK, BLK), F32)
        blocks = 2 * _nbytes((S, w), ACT) + 2 * _nbytes((BLK, w), ACT)
    else:
        c_cols, c_rows = extras
        in_specs.append(pl.BlockSpec((None, S, LANES), lambda b, h, j: (b, 0, 0)))
        in_specs.append(pl.BlockSpec((None, G, nb, BLK), lambda b, h, j: (b, h, 0, 0)))
        scratch.append(pltpu.VMEM((G, nb, BLK, LANES), F32))
        scratch_bytes += _nbytes((G, S, LANES), F32)
        blocks = (2 * _nbytes((S, w), ACT) + 2 * _nbytes((BLK, w), ACT)
                  + _nbytes((S, LANES), F32) + _nbytes((G, nb, BLK), F32))
    return pl.pallas_call(
        functools.partial(_attn_kernel, mode=mode, d=d, BLK=BLK, nb=nb, topk=MOBA_TOPK, G=G),
        grid=(B, H // G, nb),
        in_specs=in_specs,
        out_specs=pl.BlockSpec((None, BLK, w), lambda b, h, j: (b, j, h)),
        out_shape=jax.ShapeDtypeStruct((B, S, H * d), ACT),
        scratch_shapes=scratch,
        compiler_params=pltpu.CompilerParams(
            dimension_semantics=("parallel", "parallel", "arbitrary"),
            vmem_limit_bytes=_vmem_limit(blocks, scratch_bytes)),
        name="attn_" + mode,
    )(u, u, u, *extras)


def _split_in_proj(w_in, lo, hi):
    wb = lax.optimization_barrier(w_in.astype(BF16))
    main = jnp.concatenate([wb[:, :lo], wb[:, hi:]], axis=1)
    side = jnp.pad(wb[:, lo:hi], ((0, 0), (0, LANES - (hi - lo))))
    return main, side


def kernel(x, ev_norm_mix, ev_w_in, ev_gla_gate_w2, ev_gla_gate_b, ev_gla_out_norm, ev_w_out, ev_norm_mlp, ev_w_up, ev_w_down, od_norm_mix, od_w_in, od_fox_fgate_b, od_hgrn_out_norm, od_w_out, od_norm_mlp, od_w_up, od_w_down, hgrn_lb_raw, final_norm):
    B, S, D = x.shape
    T = B * S
    depth = hgrn_lb_raw.shape[0]
    lb_soft = jax.nn.softmax(hgrn_lb_raw.astype(F32), axis=0)
    lb_all = jnp.cumsum(lb_soft, axis=0) - lb_soft[0]
    slopes = jnp.exp2(-8.0 * jnp.arange(1, MOBA_HEADS + 1, dtype=F32) / MOBA_HEADS)
    slopes = jnp.broadcast_to(slopes[:, None, None], (MOBA_HEADS, 1, LANES))

    h = x.reshape(T, D)
    for layer in range(depth):
        e = layer // 2
        last = layer == depth - 1
        if layer % 2 == 0:
            gla_w = GLA_HEADS * GLA_DK
            lo = 2 * gla_w + 2 * GLA_HEADS * GLA_DV
            w_main, w_side = _split_in_proj(ev_w_in[e], lo, lo + GLA_GATE_RANK)
            u, glr = _norm_matmul(h, ev_norm_mix[e], w_main, w_side)
            u = u.reshape(B, S, -1)
            w2 = jnp.pad(ev_gla_gate_w2[e], ((0, LANES - GLA_GATE_RANK), (0, 0))).astype(BF16)
            mix_a = _linattn("gla", u, (0, 1, 1, 2),
                             [glr.reshape(B, S, LANES), w2, ev_gla_gate_b[e].reshape(1, gla_w)],
                             ev_gla_out_norm[e], B, S, GLA_HEADS, GLA_DK, GLA_DV)
            m0 = lo // MOBA_HD
            mix_b = _attention("moba", u, (m0, m0 + MOBA_HEADS, m0 + 2 * MOBA_HEADS), [slopes],
                               B, S, MOBA_HEADS, MOBA_HD)
            w_out, norm_mlp, w_up, w_down = ev_w_out[e], ev_norm_mlp[e], ev_w_up[e], ev_w_down[e]
        else:
            fox_w = FOX_HEADS * FOX_HD
            lo = 3 * fox_w
            w_main, w_side = _split_in_proj(od_w_in[e], lo, lo + FOX_HEADS)
            u, ff = _norm_matmul(h, od_norm_mix[e], w_main, w_side)
            u = u.reshape(B, S, -1)
            bias = jnp.pad(od_fox_fgate_b[e], (0, LANES - FOX_HEADS)).reshape(1, LANES)
            c_cols = _fox_gate(ff.reshape(B, S, LANES), bias, B, S)
            blk = min(ATTN_BLOCK, S)
            c_rows = c_cols[:, :, :FOX_HEADS].transpose(0, 2, 1).reshape(B, FOX_HEADS, S // blk, blk)
            mix_a = _attention("fox", u, (0, FOX_HEADS, 2 * FOX_HEADS), [c_cols, c_rows],
                               B, S, FOX_HEADS, FOX_HD)
            hw = HGRN_HEADS * HGRN_DK
            h0 = lo // hw
            mix_b = _linattn("hgrn", u, (h0, h0 + 1, h0 + 2, h0 + 3),
                             [lb_all[layer].reshape(1, hw)],
                             od_hgrn_out_norm[e], B, S, HGRN_HEADS, HGRN_DK, HGRN_DV)
            w_out, norm_mlp, w_up, w_down = od_w_out[e], od_norm_mlp[e], od_w_up[e], od_w_down[e]
        h = _out_proj(mix_a.reshape(T, -1), mix_b.reshape(T, -1), w_out.astype(BF16), h)
        h = _mlp(h, norm_mlp, w_up.astype(BF16), w_down.astype(BF16), final_norm, last)
    return h.reshape(B, S, D)
```

```python
import functools

import jax
import jax.numpy as jnp
from jax import lax
from jax.experimental import pallas as pl
from jax.experimental.pallas import tpu as pltpu

F32 = jnp.float32
BF16 = jnp.bfloat16
ACT = BF16

EPS = 1e-6
D_MODEL = 2048
D_FF = 4 * D_MODEL
GLA_HEADS, GLA_DK, GLA_DV = 4, 128, 256
GLA_GATE_RANK = 16
GLA_GATE_TAU = 16.0
MOBA_HEADS, MOBA_HD, MOBA_BLOCK, MOBA_TOPK = 8, 128, 256, 3
FOX_HEADS, FOX_HD = 8, 128
HGRN_HEADS, HGRN_DK, HGRN_DV = 8, 128, 128
LINATTN_CHUNK = 64

LANES = 128
SUBLANES = 8
MIB = 1024 * 1024
VMEM_INTERNAL_SCRATCH = 8 * MIB

PROJ_TM, PROJ_TN = 1024, 1024
OUT_TM = 512
MLP_TM, MLP_TF = 1024, 512
LINATTN_TS = 512
ATTN_BLOCK = 256
ATTN_HEADS_PER_STEP = 8
CUMSUM_BLOCK = 256
WEIGHT_PREP_ROWS = 256
LINATTN_SAFE_LOG_DECAY = 60.0
MASKED_LOGIT = -1e30
LOG2E = 1.4426950408889634

_NT = (((1,), (1,)), ((), ()))
_TN = (((0,), (0,)), ((), ()))


def _vmem_limit(pipelined_bytes, scratch_bytes=0):
    return int(2 * pipelined_bytes + scratch_bytes + VMEM_INTERNAL_SCRATCH)


def _nbytes(shape, dtype):
    n = 1
    for s in shape:
        n *= s
    return n * jnp.dtype(dtype).itemsize


def _sigmoid(x):
    return 1.0 / (1.0 + jnp.exp(-x))


def _log_sigmoid(x):
    return jnp.minimum(x, 0.0) - jnp.log1p(jnp.exp(-jnp.abs(x)))


def _rms_scale(x):
    return x * lax.rsqrt(jnp.mean(x * x, axis=-1, keepdims=True) + EPS)


def _norm_matmul_kernel(x_ref, g_ref, w_ref, ws_ref, o_ref, os_ref, xn_ref):
    @pl.when(pl.program_id(1) == 0)
    def _():
        xn = (_rms_scale(x_ref[...]) * g_ref[...]).astype(BF16)
        xn_ref[...] = xn
        os_ref[...] = jnp.dot(xn, ws_ref[...], preferred_element_type=F32)

    o_ref[...] = jnp.dot(xn_ref[...], w_ref[...], preferred_element_type=F32).astype(o_ref.dtype)


def _norm_matmul(x, g, w, ws):
    T, D = x.shape
    N = w.shape[1]
    tm, tn = min(PROJ_TM, T), PROJ_TN
    assert T % tm == 0 and N % tn == 0 and ws.shape == (D, LANES)
    blocks = (_nbytes((tm, D), F32) + _nbytes((D, tn), BF16) + _nbytes((D, LANES), BF16)
              + _nbytes((tm, tn), ACT) + _nbytes((tm, LANES), F32))
    return pl.pallas_call(
        _norm_matmul_kernel,
        grid=(T // tm, N // tn),
        in_specs=[
            pl.BlockSpec((tm, D), lambda i, j: (i, 0)),
            pl.BlockSpec((1, D), lambda i, j: (0, 0)),
            pl.BlockSpec((D, tn), lambda i, j: (0, j)),
            pl.BlockSpec((D, LANES), lambda i, j: (0, 0)),
        ],
        out_specs=[
            pl.BlockSpec((tm, tn), lambda i, j: (i, j)),
            pl.BlockSpec((tm, LANES), lambda i, j: (i, 0)),
        ],
        out_shape=[jax.ShapeDtypeStruct((T, N), ACT), jax.ShapeDtypeStruct((T, LANES), F32)],
        scratch_shapes=[pltpu.VMEM((tm, D), BF16)],
        compiler_params=pltpu.CompilerParams(
            dimension_semantics=("parallel", "arbitrary"),
            vmem_limit_bytes=_vmem_limit(blocks, _nbytes((tm, D), BF16))),
        name="norm_in_proj",
    )(x, g.reshape(1, D), w, ws)


def _out_proj_kernel(a_ref, b_ref, wa_ref, wb_ref, h_ref, o_ref):
    acc = jnp.dot(a_ref[...].astype(BF16), wa_ref[...], preferred_element_type=F32)
    acc += jnp.dot(b_ref[...].astype(BF16), wb_ref[...], preferred_element_type=F32)
    o_ref[...] = h_ref[...] + acc


def _out_proj(a, b, w, h):
    T, D = h.shape
    Ka, Kb = a.shape[1], b.shape[1]
    tm = min(OUT_TM, T)
    assert T % tm == 0 and Ka == Kb and w.shape == (Ka + Kb, D)
    blocks = (_nbytes((tm, Ka), a.dtype) + _nbytes((tm, Kb), b.dtype) + _nbytes((Ka, D), BF16)
              + _nbytes((Kb, D), BF16) + 2 * _nbytes((tm, D), F32))
    return pl.pallas_call(
        _out_proj_kernel,
        grid=(T // tm,),
        in_specs=[
            pl.BlockSpec((tm, Ka), lambda i: (i, 0)),
            pl.BlockSpec((tm, Kb), lambda i: (i, 0)),
            pl.BlockSpec((Ka, D), lambda i: (0, 0)),
            pl.BlockSpec((Kb, D), lambda i: (1, 0)),
            pl.BlockSpec((tm, D), lambda i: (i, 0)),
        ],
        out_specs=pl.BlockSpec((tm, D), lambda i: (i, 0)),
        out_shape=jax.ShapeDtypeStruct((T, D), F32),
        compiler_params=pltpu.CompilerParams(
            dimension_semantics=("parallel",), vmem_limit_bytes=_vmem_limit(blocks)),
        name="out_proj",
    )(a, b, w, w, h)


def _mlp_kernel(h_ref, g_ref, wu_ref, wd_ref, gf_ref, o_ref, xn_ref, *, final_norm):
    f = pl.program_id(1)

    @pl.when(f == 0)
    def _():
        x = h_ref[...]
        xn_ref[...] = (_rms_scale(x) * g_ref[...]).astype(BF16)
        o_ref[...] = x

    hid = jnp.dot(xn_ref[...], wu_ref[...], preferred_element_type=F32)
    hid = jnp.square(jnp.maximum(hid, 0.0))
    o_ref[...] += jnp.dot(hid.astype(BF16), wd_ref[...], preferred_element_type=F32)

    if final_norm:
        @pl.when(f == pl.num_programs(1) - 1)
        def _():
            o_ref[...] = _rms_scale(o_ref[...]) * gf_ref[...]


def _mlp(h, g, wu, wd, gf, final_norm):
    T, D = h.shape
    FF = wu.shape[1]
    tm, tf = min(MLP_TM, T), MLP_TF
    assert T % tm == 0 and FF % tf == 0
    blocks = (2 * _nbytes((tm, D), F32) + _nbytes((D, tf), BF16) + _nbytes((tf, D), BF16))
    return pl.pallas_call(
        functools.partial(_mlp_kernel, final_norm=final_norm),
        grid=(T // tm, FF // tf),
        in_specs=[
            pl.BlockSpec((tm, D), lambda i, f: (i, 0)),
            pl.BlockSpec((1, D), lambda i, f: (0, 0)),
            pl.BlockSpec((D, tf), lambda i, f: (0, f)),
            pl.BlockSpec((tf, D), lambda i, f: (f, 0)),
            pl.BlockSpec((1, D), lambda i, f: (0, 0)),
        ],
        out_specs=pl.BlockSpec((tm, D), lambda i, f: (i, 0)),
        out_shape=jax.ShapeDtypeStruct((T, D), F32),
        scratch_shapes=[pltpu.VMEM((tm, D), BF16)],
        compiler_params=pltpu.CompilerParams(
            dimension_semantics=("parallel", "arbitrary"),
            vmem_limit_bytes=_vmem_limit(blocks, _nbytes((tm, D), BF16))),
        name="mlp",
    )(h, g.reshape(1, D), wu, wd, gf.reshape(1, D))


def _linattn_kernel(*refs, mode, H, dk, dv, C, Ts):
    if mode == "gla":
        (q_ref, k_ref, v_ref, g_ref, glr_ref, w2_ref, b2_ref, gain_ref,
         o_ref, st_ref, b_sc, q_sc, k_sc) = refs
    else:
        (q_ref, k_ref, v_ref, g_ref, lb_ref, gain_ref,
         o_ref, st_ref, b_sc, q_sc, k_sc) = refs

    @pl.when(pl.program_id(1) == 0)
    def _():
        st_ref[...] = jnp.zeros_like(st_ref)

    n_chunks = Ts // C
    pos = lax.broadcasted_iota(jnp.int32, (C, H * dk), 0)

    def prepare(c, b_min):
        rows = pl.ds(pl.multiple_of(c * C, C), C)
        if mode == "gla":
            z = jnp.dot(glr_ref[rows, :].astype(BF16), w2_ref[...],
                        preferred_element_type=F32) + b2_ref[...]
            b = _log_sigmoid(z) * (1.0 / GLA_GATE_TAU)
            q_sc[rows, :] = q_ref[rows, :].astype(F32) * (dk ** -0.5)
            k_sc[rows, :] = k_ref[rows, :].astype(F32)
        else:
            lb = lb_ref[...]
            f = lb + (1.0 - lb) * _sigmoid(k_ref[rows, :].astype(F32))
            b = jnp.log(f)
            k_sc[rows, :] = 1.0 - f
            x = q_ref[rows, :].astype(F32)
            q_sc[rows, :] = x * _sigmoid(x)
        shift = 1
        while shift < C:
            b = b + jnp.where(pos >= shift, pltpu.roll(b, shift, axis=0), 0.0)
            shift *= 2
        b_sc[rows, :] = b
        return jnp.minimum(b_min, b[C - 1:C, :])

    b_min = lax.fori_loop(0, n_chunks, prepare, jnp.zeros((1, H * dk), F32))
    factorable = jnp.min(b_min) >= -LINATTN_SAFE_LOG_DECAY

    row = lax.broadcasted_iota(jnp.int32, (C, C), 0)
    col = lax.broadcasted_iota(jnp.int32, (C, C), 1)
    gain = gain_ref[...]

    def chunk(c, direct):
        r0 = pl.multiple_of(c * C, C)
        rows = pl.ds(r0, C)
        inter, scores, vs, new_state = [], [], [], []
        for h in range(H):
            ck = slice(h * dk, (h + 1) * dk)
            b = b_sc[rows, ck]
            q = q_sc[rows, ck]
            k = k_sc[rows, ck]
            v = v_ref[rows, h * dv:(h + 1) * dv].astype(BF16)
            b_last = b[C - 1:C, :]
            qe = (q * jnp.exp(b)).astype(BF16)
            state = st_ref[h]
            inter.append(lax.dot_general(qe, state.astype(BF16), _NT, preferred_element_type=F32))
            if direct:
                def columns(s8, sc, ck=ck, b=b, q=q):
                    r8 = pl.ds(pl.multiple_of(r0 + s8 * SUBLANES, SUBLANES), SUBLANES)
                    b8 = b_sc[r8, ck]
                    k8 = k_sc[r8, ck]
                    for i in range(SUBLANES):
                        decay = jnp.exp(b - b8[i:i + 1, :])
                        contrib = jnp.sum(q * (k8[i:i + 1, :] * decay), axis=-1, keepdims=True)
                        sc = jnp.where(col == s8 * SUBLANES + i, contrib, sc)
                    return sc
                sc = lax.fori_loop(0, C // SUBLANES, columns, jnp.zeros((C, C), F32))
            else:
                ke = (k * jnp.exp(-b)).astype(BF16)
                sc = lax.dot_general(qe, ke, _NT, preferred_element_type=F32)
            scores.append(sc)
            k_tail = (k * jnp.exp(b_last - b)).astype(BF16)
            new_state.append(state * jnp.exp(b_last) + lax.dot_general(
                v, k_tail, _TN, preferred_element_type=F32))
            vs.append(v)
        outs = []
        for h in range(H):
            sc = jnp.where(row >= col, scores[h], 0.0).astype(BF16)
            outs.append(inter[h] + jnp.dot(sc, vs[h], preferred_element_type=F32))
        for h in range(H):
            cv = slice(h * dv, (h + 1) * dv)
            st_ref[h] = new_state[h]
            g = g_ref[rows, cv].astype(F32)
            o_ref[rows, cv] = ((_rms_scale(outs[h]) * gain) * (g * _sigmoid(g))).astype(o_ref.dtype)

    @pl.when(factorable)
    def _():
        lax.fori_loop(0, n_chunks, lambda c, _: (chunk(c, False), 0)[1], 0)

    @pl.when(jnp.logical_not(factorable))
    def _():
        lax.fori_loop(0, n_chunks, lambda c, _: (chunk(c, True), 0)[1], 0)


def _linattn(mode, u, col_blocks, extras, gain, B, S, H, dk, dv):
    C = LINATTN_CHUNK
    Ts = min(LINATTN_TS, S)
    assert S % Ts == 0 and Ts % C == 0
    wk, wv = H * dk, H * dv
    qi, ki, vi, gi = col_blocks
    in_specs = [
        pl.BlockSpec((None, Ts, wk), lambda b, s: (b, s, qi)),
        pl.BlockSpec((None, Ts, wk), lambda b, s: (b, s, ki)),
        pl.BlockSpec((None, Ts, wv), lambda b, s: (b, s, vi)),
        pl.BlockSpec((None, Ts, wv), lambda b, s: (b, s, gi)),
    ]
    args = [u, u, u, u]
    for e in extras:
        if e.ndim == 3:
            in_specs.append(pl.BlockSpec((None, Ts, e.shape[2]), lambda b, s: (b, s, 0)))
        else:
            in_specs.append(pl.BlockSpec(e.shape, lambda b, s: (0, 0)))
        args.append(e)
    in_specs.append(pl.BlockSpec((1, dv), lambda b, s: (0, 0)))
    args.append(gain.reshape(1, dv))
    blocks = (2 * _nbytes((Ts, wk), ACT) + 3 * _nbytes((Ts, wv), ACT) + _nbytes((Ts, LANES), F32)
              + _nbytes((LANES, wk), BF16))
    scratch = 3 * _nbytes((Ts, wk), F32) + _nbytes((H, dv, dk), F32)
    return pl.pallas_call(
        functools.partial(_linattn_kernel, mode=mode, H=H, dk=dk, dv=dv, C=C, Ts=Ts),
        grid=(B, S // Ts),
        in_specs=in_specs,
        out_specs=pl.BlockSpec((None, Ts, wv), lambda b, s: (b, s, 0)),
        out_shape=jax.ShapeDtypeStruct((B, S, wv), ACT),
        scratch_shapes=[
            pltpu.VMEM((H, dv, dk), F32),
            pltpu.VMEM((Ts, wk), F32),
            pltpu.VMEM((Ts, wk), F32),
            pltpu.VMEM((Ts, wk), F32),
        ],
        compiler_params=pltpu.CompilerParams(
            dimension_semantics=("parallel", "arbitrary"),
            vmem_limit_bytes=_vmem_limit(blocks, scratch)),
        name="linattn_" + mode,
    )(*args)


def _fox_gate_kernel(ff_ref, bias_ref, c_ref, *, S, CB):
    r = lax.broadcasted_iota(jnp.int32, (CB, CB), 0)
    c = lax.broadcasted_iota(jnp.int32, (CB, CB), 1)
    tril = jnp.where(r >= c, 1.0, 0.0).astype(F32)
    carry = jnp.zeros((1, LANES), F32)
    for i in range(S // CB):
        rows = slice(i * CB, (i + 1) * CB)
        log_f = _log_sigmoid(ff_ref[rows, :] + bias_ref[...])
        cs = jnp.dot(tril, log_f, precision=lax.Precision.HIGHEST,
                     preferred_element_type=F32) + carry
        c_ref[rows, :] = cs
        carry = cs[CB - 1:CB, :]


def _fox_gate(ff, bias, B, S):
    CB = min(CUMSUM_BLOCK, S)
    assert S % CB == 0
    return pl.pallas_call(
        functools.partial(_fox_gate_kernel, S=S, CB=CB),
        grid=(B,),
        in_specs=[
            pl.BlockSpec((None, S, LANES), lambda b: (b, 0, 0)),
            pl.BlockSpec((1, LANES), lambda b: (0, 0)),
        ],
        out_specs=pl.BlockSpec((None, S, LANES), lambda b: (b, 0, 0)),
        out_shape=jax.ShapeDtypeStruct((B, S, LANES), F32),
        compiler_params=pltpu.CompilerParams(
            dimension_semantics=("parallel",),
            vmem_limit_bytes=_vmem_limit(2 * _nbytes((S, LANES), F32))),
        name="fox_gate",
    )(ff, bias)


def _attn_kernel(*refs, mode, d, BLK, nb, topk, G):
    if mode == "moba":
        q_ref, k_ref, v_ref, slope_ref, o_ref, vt_sc, acc_sc, kmean_sc, relb_sc = refs
    else:
        q_ref, k_ref, v_ref, ccol_ref, crow_ref, o_ref, vt_sc, acc_sc, crep_sc = refs
    assert q_ref.dtype == BF16 and k_ref.dtype == BF16 and v_ref.dtype == BF16
    hg = pl.program_id(1)
    j = pl.program_id(2)
    S = nb * BLK
    key_i = lax.broadcasted_iota(jnp.int32, (BLK, BLK), 0)
    qry_i = lax.broadcasted_iota(jnp.int32, (BLK, BLK), 1)

    @pl.when(j == 0)
    def _():
        for g in range(G):
            cd = slice(g * d, (g + 1) * d)
            if mode == "moba":
                kmean_sc[g] = jnp.zeros(kmean_sc.shape[1:], F32)
                relb_sc[g] = (slope_ref[g, :, :1] * LOG2E) * (qry_i - key_i).astype(F32)
            else:
                lane = lax.broadcasted_iota(jnp.int32, (S, LANES), 1)
                c_key = jnp.sum(jnp.where(lane == hg * G + g, ccol_ref[...], 0.0),
                                axis=-1, keepdims=True)
            for n in range(nb):
                rows = slice(n * BLK, (n + 1) * BLK)
                if mode == "fox":
                    crep_sc[g, n] = jnp.broadcast_to(c_key[rows] * LOG2E, (BLK, LANES))
                vt_sc[g, n] = v_ref[rows, cd].astype(F32).T.astype(BF16)
                if mode == "moba":
                    kmean_sc[g, n:n + 1, :] = jnp.mean(k_ref[rows, cd].astype(F32),
                                                       axis=0, keepdims=True)

    causal = qry_i >= key_i
    qb, slope, selected = [], [], []
    if mode == "moba":
        blk_i = lax.broadcasted_iota(jnp.int32, (kmean_sc.shape[1], BLK), 0)

    for g in range(G):
        q_raw = q_ref[:, g * d:(g + 1) * d]
        qb.append((q_raw.astype(F32) * (d ** -0.5 * LOG2E)).astype(BF16))
        if mode == "moba":
            kmean = kmean_sc[g]
            hi = kmean.astype(BF16).astype(F32)
            mid = (kmean - hi).astype(BF16).astype(F32)
            lo = (kmean - hi) - mid
            nbp = kmean.shape[0]
            parts = lax.dot_general(jnp.concatenate([hi, mid, lo], axis=0).astype(BF16), q_raw,
                                    _NT, preferred_element_type=F32)
            gate = (parts[:nbp] + parts[nbp:2 * nbp] + parts[2 * nbp:]) * (d ** -0.5)
            rank = jnp.zeros(gate.shape, F32)
            for m in range(nb):
                gm = gate[m:m + 1, :]
                beats = jnp.where(gm > gate, 1.0,
                                  jnp.where((gm == gate) & (blk_i > m), 1.0, 0.0))
                rank = rank + beats * (m < j).astype(F32)
            selected.append(jnp.where((blk_i < j) & (rank < topk), 1.0, 0.0))
            slope.append(slope_ref[g, :, :1] * LOG2E)

    def partial(g, n):
        k_blk = k_ref[pl.ds(pl.multiple_of(n * BLK, BLK), BLK), g * d:(g + 1) * d]
        s = lax.dot_general(k_blk, qb[g], _NT, preferred_element_type=F32)
        if mode == "moba":
            return s - relb_sc[g]
        return s - jnp.concatenate([crep_sc[g, n]] * (BLK // LANES), axis=1)

    def adjust(g, n):
        if mode == "moba":
            return -(slope[g] * ((j - n) * BLK).astype(F32))
        return crow_ref[g, pl.ds(j, 1), :] * LOG2E

    scores = [jnp.where(causal, partial(g, j), MASKED_LOGIT) for g in range(G)]
    stats, probs = [], []
    for g in range(G):
        m0 = jnp.max(scores[g], axis=0, keepdims=True)
        p = jnp.exp2(scores[g] - m0)
        stats += [m0 + adjust(g, j), jnp.sum(p, axis=0, keepdims=True)]
        probs.append(p.astype(BF16))
    for g in range(G):
        acc_sc[g] = jnp.dot(vt_sc[g, j], probs[g], preferred_element_type=F32)

    def past_block(n, stats):
        scores = [partial(g, n) for g in range(G)]
        out, probs, alphas = [], [], []
        for g in range(G):
            m_prev, l_prev = stats[2 * g], stats[2 * g + 1]
            s = scores[g]
            if mode == "moba":
                keep = jnp.sum(jnp.where(blk_i == n, selected[g], 0.0), axis=0, keepdims=True)
                s = jnp.where(keep > 0.5, s, MASKED_LOGIT)
            adj = adjust(g, n)
            m_new = jnp.maximum(m_prev, jnp.max(s, axis=0, keepdims=True) + adj)
            alpha = jnp.exp2(m_prev - m_new)
            p = jnp.exp2(s - (m_new - adj))
            out += [m_new, alpha * l_prev + jnp.sum(p, axis=0, keepdims=True)]
            probs.append(p.astype(BF16))
            alphas.append(alpha)
        updates = [jnp.dot(vt_sc[g, n], probs[g], preferred_element_type=F32) for g in range(G)]
        for g in range(G):
            acc_sc[g] = alphas[g] * acc_sc[g] + updates[g]
        return tuple(out)

    stats = lax.fori_loop(0, j, past_block, tuple(stats))
    for g in range(G):
        o_ref[:, g * d:(g + 1) * d] = (acc_sc[g] / stats[2 * g + 1]).T.astype(o_ref.dtype)


def _attention(mode, u, col_blocks, extras, B, S, H, d):
    BLK = min(ATTN_BLOCK, S)
    G = ATTN_HEADS_PER_STEP
    assert S % BLK == 0 and BLK % LANES == 0 and H % G == 0
    nb = S // BLK
    nbp = -(-nb // SUBLANES) * SUBLANES
    qi, ki, vi = (c // G for c in col_blocks)
    assert all(c % G == 0 for c in col_blocks)
    w = G * d
    in_specs = [
        pl.BlockSpec((None, BLK, w), lambda b, h, j: (b, j, qi + h)),
        pl.BlockSpec((None, S, w), lambda b, h, j: (b, 0, ki + h)),
        pl.BlockSpec((None, S, w), lambda b, h, j: (b, 0, vi + h)),
    ]
    scratch = [pltpu.VMEM((G, nb, d, BLK), BF16), pltpu.VMEM((G, d, BLK), F32)]
    scratch_bytes = _nbytes((G, S, d), BF16) + _nbytes((G, d, BLK), F32)
    if mode == "moba":
        (slopes,) = extras
        in_specs.append(pl.BlockSpec((G, 1, LANES), lambda b, h, j: (h, 0, 0)))
        scratch += [pltpu.VMEM((G, nbp, d), F32), pltpu.VMEM((G, BLK, BLK), F32)]
        scratch_bytes += _nbytes((G, BLK, BLK), F32)
        blocks = 2 * _nbytes((S, w), ACT) + 2 * _nbytes((BLK, w), ACT)
    else:
        c_cols, c_rows = extras
        in_specs.append(pl.BlockSpec((None, S, LANES), lambda b, h, j: (b, 0, 0)))
        in_specs.append(pl.BlockSpec((None, G, nb, BLK), lambda b, h, j: (b, h, 0, 0)))
        scratch.append(pltpu.VMEM((G, nb, BLK, LANES), F32))
        scratch_bytes += _nbytes((G, S, LANES), F32)
        blocks = (2 * _nbytes((S, w), ACT) + 2 * _nbytes((BLK, w), ACT)
                  + _nbytes((S, LANES), F32) + _nbytes((G, nb, BLK), F32))
    return pl.pallas_call(
        functools.partial(_attn_kernel, mode=mode, d=d, BLK=BLK, nb=nb, topk=MOBA_TOPK, G=G),
        grid=(B, H // G, nb),
        in_specs=in_specs,
        out_specs=pl.BlockSpec((None, BLK, w), lambda b, h, j: (b, j, h)),
        out_shape=jax.ShapeDtypeStruct((B, S, H * d), ACT),
        scratch_shapes=scratch,
        compiler_params=pltpu.CompilerParams(
            dimension_semantics=("parallel", "parallel", "arbitrary"),
            vmem_limit_bytes=_vmem_limit(blocks, scratch_bytes)),
        name="attn_" + mode,
    )(u, u, u, *extras)


def _split_in_proj_kernel(w_ref, main_ref, side_ref, *, lo, hi):
    main_ref[:, :lo] = w_ref[:, :lo].astype(BF16)
    tail = w_ref[:, lo:]
    main_ref[:, lo:] = tail[:, hi - lo:].astype(BF16)
    lane = lax.broadcasted_iota(jnp.int32, side_ref.shape, 1)
    side_ref[...] = jnp.where(lane < hi - lo, tail[:, :LANES], 0.0).astype(BF16)


def _split_in_proj(w_in, lo, hi):
    D, n_in = w_in.shape
    n_main = n_in - (hi - lo)
    tr = WEIGHT_PREP_ROWS
    assert D % tr == 0 and lo % LANES == 0 and hi - lo <= LANES <= n_in - lo
    blocks = _nbytes((tr, n_in), F32) + _nbytes((tr, n_main), BF16) + _nbytes((tr, LANES), BF16)
    return pl.pallas_call(
        functools.partial(_split_in_proj_kernel, lo=lo, hi=hi),
        grid=(D // tr,),
        in_specs=[pl.BlockSpec((tr, n_in), lambda i: (i, 0))],
        out_specs=[pl.BlockSpec((tr, n_main), lambda i: (i, 0)),
                   pl.BlockSpec((tr, LANES), lambda i: (i, 0))],
        out_shape=[jax.ShapeDtypeStruct((D, n_main), BF16), jax.ShapeDtypeStruct((D, LANES), BF16)],
        compiler_params=pltpu.CompilerParams(
            dimension_semantics=("parallel",), vmem_limit_bytes=_vmem_limit(blocks)),
        name="split_in_proj",
    )(w_in)


def kernel(x, ev_norm_mix, ev_w_in, ev_gla_gate_w2, ev_gla_gate_b, ev_gla_out_norm, ev_w_out, ev_norm_mlp, ev_w_up, ev_w_down, od_norm_mix, od_w_in, od_fox_fgate_b, od_hgrn_out_norm, od_w_out, od_norm_mlp, od_w_up, od_w_down, hgrn_lb_raw, final_norm):
    B, S, D = x.shape
    T = B * S
    depth = hgrn_lb_raw.shape[0]
    lb_soft = jax.nn.softmax(hgrn_lb_raw.astype(F32), axis=0)
    lb_all = jnp.cumsum(lb_soft, axis=0) - lb_soft[0]
    slopes = jnp.exp2(-8.0 * jnp.arange(1, MOBA_HEADS + 1, dtype=F32) / MOBA_HEADS)
    slopes = jnp.broadcast_to(slopes[:, None, None], (MOBA_HEADS, 1, LANES))

    h = x.reshape(T, D)
    for layer in range(depth):
        e = layer // 2
        last = layer == depth - 1
        if layer % 2 == 0:
            gla_w = GLA_HEADS * GLA_DK
            lo = 2 * gla_w + 2 * GLA_HEADS * GLA_DV
            w_main, w_side = _split_in_proj(ev_w_in[e], lo, lo + GLA_GATE_RANK)
            u, glr = _norm_matmul(h, ev_norm_mix[e], w_main, w_side)
            u = u.reshape(B, S, -1)
            w2 = jnp.pad(ev_gla_gate_w2[e], ((0, LANES - GLA_GATE_RANK), (0, 0))).astype(BF16)
            mix_a = _linattn("gla", u, (0, 1, 1, 2),
                             [glr.reshape(B, S, LANES), w2, ev_gla_gate_b[e].reshape(1, gla_w)],
                             ev_gla_out_norm[e], B, S, GLA_HEADS, GLA_DK, GLA_DV)
            m0 = lo // MOBA_HD
            mix_b = _attention("moba", u, (m0, m0 + MOBA_HEADS, m0 + 2 * MOBA_HEADS), [slopes],
                               B, S, MOBA_HEADS, MOBA_HD)
            w_out, norm_mlp, w_up, w_down = ev_w_out[e], ev_norm_mlp[e], ev_w_up[e], ev_w_down[e]
        else:
            fox_w = FOX_HEADS * FOX_HD
            lo = 3 * fox_w
            w_main, w_side = _split_in_proj(od_w_in[e], lo, lo + FOX_HEADS)
            u, ff = _norm_matmul(h, od_norm_mix[e], w_main, w_side)
            u = u.reshape(B, S, -1)
            bias = jnp.pad(od_fox_fgate_b[e], (0, LANES - FOX_HEADS)).reshape(1, LANES)
            c_cols = _fox_gate(ff.reshape(B, S, LANES), bias, B, S)
            blk = min(ATTN_BLOCK, S)
            c_rows = c_cols[:, :, :FOX_HEADS].transpose(0, 2, 1).reshape(B, FOX_HEADS, S // blk, blk)
            mix_a = _attention("fox", u, (0, FOX_HEADS, 2 * FOX_HEADS), [c_cols, c_rows],
                               B, S, FOX_HEADS, FOX_HD)
            hw = HGRN_HEADS * HGRN_DK
            h0 = lo // hw
            mix_b = _linattn("hgrn", u, (h0, h0 + 1, h0 + 2, h0 + 3),
                             [lb_all[layer].reshape(1, hw)],
                             od_hgrn_out_norm[e], B, S, HGRN_HEADS, HGRN_DK, HGRN_DV)
            w_out, norm_mlp, w_up, w_down = od_w_out[e], od_norm_mlp[e], od_w_up[e], od_w_down[e]
        h = _out_proj(mix_a.reshape(T, -1), mix_b.reshape(T, -1), w_out.astype(BF16), h)
        h = _mlp(h, norm_mlp, w_up.astype(BF16), w_down.astype(BF16), final_norm, last)
    return h.reshape(B, S, D)
```

```python
import functools

import jax
import jax.numpy as jnp
from jax import lax
from jax.experimental import pallas as pl
from jax.experimental.pallas import tpu as pltpu

F32 = jnp.float32
BF16 = jnp.bfloat16
ACT = BF16

EPS = 1e-6
D_MODEL = 2048
D_FF = 4 * D_MODEL
GLA_HEADS, GLA_DK, GLA_DV = 4, 128, 256
GLA_GATE_RANK = 16
GLA_GATE_TAU = 16.0
MOBA_HEADS, MOBA_HD, MOBA_BLOCK, MOBA_TOPK = 8, 128, 256, 3
FOX_HEADS, FOX_HD = 8, 128
HGRN_HEADS, HGRN_DK, HGRN_DV = 8, 128, 128
LINATTN_CHUNK = 64

LANES = 128
SUBLANES = 8
MIB = 1024 * 1024
VMEM_INTERNAL_SCRATCH = 8 * MIB

PROJ_TM, PROJ_TN = 1024, 1024
OUT_TM = 512
MLP_TM, MLP_TF = 1024, 512
LINATTN_TS = 512
ATTN_BLOCK = 256
ATTN_HEADS_PER_STEP = 8
CUMSUM_BLOCK = 256
LINATTN_SAFE_LOG_DECAY = 60.0
MASKED_LOGIT = -1e30
LOG2E = 1.4426950408889634

_NT = (((1,), (1,)), ((), ()))
_TN = (((0,), (0,)), ((), ()))


def _vmem_limit(pipelined_bytes, scratch_bytes=0):
    return int(2 * pipelined_bytes + scratch_bytes + VMEM_INTERNAL_SCRATCH)


def _nbytes(shape, dtype):
    n = 1
    for s in shape:
        n *= s
    return n * jnp.dtype(dtype).itemsize


def _sigmoid(x):
    return 1.0 / (1.0 + jnp.exp(-x))


def _log_sigmoid(x):
    return jnp.minimum(x, 0.0) - jnp.log1p(jnp.exp(-jnp.abs(x)))


def _rms_scale(x):
    return x * lax.rsqrt(jnp.mean(x * x, axis=-1, keepdims=True) + EPS)


def _norm_matmul_kernel(x_ref, g_ref, wh_ref, wt_ref, ws_ref, o_ref, os_ref, xn_ref, *, n_head):
    j = pl.program_id(1)

    @pl.when(j == 0)
    def _():
        xn = (_rms_scale(x_ref[...]) * g_ref[...]).astype(BF16)
        xn_ref[...] = xn
        os_ref[...] = jnp.dot(xn, ws_ref[...], preferred_element_type=F32)

    @pl.when(j < n_head)
    def _():
        o_ref[...] = jnp.dot(xn_ref[...], wh_ref[...],
                             preferred_element_type=F32).astype(o_ref.dtype)

    @pl.when(j >= n_head)
    def _():
        o_ref[...] = jnp.dot(xn_ref[...], wt_ref[...],
                             preferred_element_type=F32).astype(o_ref.dtype)


def _norm_matmul(x, g, w_head, lo, w_tail, ws):
    T, D = x.shape
    tm, tn = min(PROJ_TM, T), PROJ_TN
    assert lo % tn == 0 and w_tail.shape[1] % tn == 0 and lo <= w_head.shape[1]
    n_head = lo // tn
    N = lo + w_tail.shape[1]
    assert T % tm == 0 and ws.shape == (D, LANES)
    blocks = (_nbytes((tm, D), F32) + 2 * _nbytes((D, tn), BF16) + _nbytes((D, LANES), BF16)
              + _nbytes((tm, tn), ACT) + _nbytes((tm, LANES), F32))
    return pl.pallas_call(
        functools.partial(_norm_matmul_kernel, n_head=n_head),
        grid=(T // tm, N // tn),
        in_specs=[
            pl.BlockSpec((tm, D), lambda i, j: (i, 0)),
            pl.BlockSpec((1, D), lambda i, j: (0, 0)),
            pl.BlockSpec((D, tn), lambda i, j: (0, jnp.minimum(j, n_head - 1))),
            pl.BlockSpec((D, tn), lambda i, j: (0, jnp.maximum(j - n_head, 0))),
            pl.BlockSpec((D, LANES), lambda i, j: (0, 0)),
        ],
        out_specs=[
            pl.BlockSpec((tm, tn), lambda i, j: (i, j)),
            pl.BlockSpec((tm, LANES), lambda i, j: (i, 0)),
        ],
        out_shape=[jax.ShapeDtypeStruct((T, N), ACT), jax.ShapeDtypeStruct((T, LANES), F32)],
        scratch_shapes=[pltpu.VMEM((tm, D), BF16)],
        compiler_params=pltpu.CompilerParams(
            dimension_semantics=("parallel", "arbitrary"),
            vmem_limit_bytes=_vmem_limit(blocks, _nbytes((tm, D), BF16))),
        name="norm_in_proj",
    )(x, g.reshape(1, D), w_head, w_tail, ws)


def _out_proj_kernel(a_ref, b_ref, wa_ref, wb_ref, h_ref, o_ref):
    acc = jnp.dot(a_ref[...].astype(BF16), wa_ref[...], preferred_element_type=F32)
    acc += jnp.dot(b_ref[...].astype(BF16), wb_ref[...], preferred_element_type=F32)
    o_ref[...] = h_ref[...] + acc


def _out_proj(a, b, w, h):
    T, D = h.shape
    Ka, Kb = a.shape[1], b.shape[1]
    tm = min(OUT_TM, T)
    assert T % tm == 0 and Ka == Kb and w.shape == (Ka + Kb, D)
    blocks = (_nbytes((tm, Ka), a.dtype) + _nbytes((tm, Kb), b.dtype) + _nbytes((Ka, D), BF16)
              + _nbytes((Kb, D), BF16) + 2 * _nbytes((tm, D), F32))
    return pl.pallas_call(
        _out_proj_kernel,
        grid=(T // tm,),
        in_specs=[
            pl.BlockSpec((tm, Ka), lambda i: (i, 0)),
            pl.BlockSpec((tm, Kb), lambda i: (i, 0)),
            pl.BlockSpec((Ka, D), lambda i: (0, 0)),
            pl.BlockSpec((Kb, D), lambda i: (1, 0)),
            pl.BlockSpec((tm, D), lambda i: (i, 0)),
        ],
        out_specs=pl.BlockSpec((tm, D), lambda i: (i, 0)),
        out_shape=jax.ShapeDtypeStruct((T, D), F32),
        compiler_params=pltpu.CompilerParams(
            dimension_semantics=("parallel",), vmem_limit_bytes=_vmem_limit(blocks)),
        name="out_proj",
    )(a, b, w, w, h)


def _mlp_kernel(h_ref, g_ref, wu_ref, wd_ref, gf_ref, o_ref, xn_ref, *, final_norm):
    f = pl.program_id(1)

    @pl.when(f == 0)
    def _():
        x = h_ref[...]
        xn_ref[...] = (_rms_scale(x) * g_ref[...]).astype(BF16)
        o_ref[...] = x

    hid = jnp.dot(xn_ref[...], wu_ref[...], preferred_element_type=F32)
    hid = jnp.square(jnp.maximum(hid, 0.0))
    o_ref[...] += jnp.dot(hid.astype(BF16), wd_ref[...], preferred_element_type=F32)

    if final_norm:
        @pl.when(f == pl.num_programs(1) - 1)
        def _():
            o_ref[...] = _rms_scale(o_ref[...]) * gf_ref[...]


def _mlp(h, g, wu, wd, gf, final_norm):
    T, D = h.shape
    FF = wu.shape[1]
    tm, tf = min(MLP_TM, T), MLP_TF
    assert T % tm == 0 and FF % tf == 0
    blocks = (2 * _nbytes((tm, D), F32) + _nbytes((D, tf), BF16) + _nbytes((tf, D), BF16))
    return pl.pallas_call(
        functools.partial(_mlp_kernel, final_norm=final_norm),
        grid=(T // tm, FF // tf),
        in_specs=[
            pl.BlockSpec((tm, D), lambda i, f: (i, 0)),
            pl.BlockSpec((1, D), lambda i, f: (0, 0)),
            pl.BlockSpec((D, tf), lambda i, f: (0, f)),
            pl.BlockSpec((tf, D), lambda i, f: (f, 0)),
            pl.BlockSpec((1, D), lambda i, f: (0, 0)),
        ],
        out_specs=pl.BlockSpec((tm, D), lambda i, f: (i, 0)),
        out_shape=jax.ShapeDtypeStruct((T, D), F32),
        scratch_shapes=[pltpu.VMEM((tm, D), BF16)],
        compiler_params=pltpu.CompilerParams(
            dimension_semantics=("parallel", "arbitrary"),
            vmem_limit_bytes=_vmem_limit(blocks, _nbytes((tm, D), BF16))),
        name="mlp",
    )(h, g.reshape(1, D), wu, wd, gf.reshape(1, D))


def _linattn_kernel(*refs, mode, H, dk, dv, C, Ts):
    if mode == "gla":
        (q_ref, k_ref, v_ref, g_ref, glr_ref, w2_ref, b2_ref, gain_ref,
         o_ref, st_ref, b_sc, q_sc, k_sc) = refs
    else:
        (q_ref, k_ref, v_ref, g_ref, lb_ref, gain_ref,
         o_ref, st_ref, b_sc, q_sc, k_sc) = refs

    @pl.when(pl.program_id(1) == 0)
    def _():
        st_ref[...] = jnp.zeros_like(st_ref)

    n_chunks = Ts // C
    pos = lax.broadcasted_iota(jnp.int32, (C, H * dk), 0)

    def prepare(c, b_min):
        rows = pl.ds(pl.multiple_of(c * C, C), C)
        if mode == "gla":
            z = jnp.dot(glr_ref[rows, :].astype(BF16), w2_ref[...],
                        preferred_element_type=F32) + b2_ref[...]
            b = _log_sigmoid(z) * (1.0 / GLA_GATE_TAU)
            q_sc[rows, :] = q_ref[rows, :].astype(F32) * (dk ** -0.5)
            k_sc[rows, :] = k_ref[rows, :].astype(F32)
        else:
            lb = lb_ref[...]
            f = lb + (1.0 - lb) * _sigmoid(k_ref[rows, :].astype(F32))
            b = jnp.log(f)
            k_sc[rows, :] = 1.0 - f
            x = q_ref[rows, :].astype(F32)
            q_sc[rows, :] = x * _sigmoid(x)
        shift = 1
        while shift < C:
            b = b + jnp.where(pos >= shift, pltpu.roll(b, shift, axis=0), 0.0)
            shift *= 2
        b_sc[rows, :] = b
        return jnp.minimum(b_min, b[C - 1:C, :])

    b_min = lax.fori_loop(0, n_chunks, prepare, jnp.zeros((1, H * dk), F32))
    factorable = jnp.min(b_min) >= -LINATTN_SAFE_LOG_DECAY

    row = lax.broadcasted_iota(jnp.int32, (C, C), 0)
    col = lax.broadcasted_iota(jnp.int32, (C, C), 1)
    gain = gain_ref[...]

    def chunk(c, direct):
        r0 = pl.multiple_of(c * C, C)
        rows = pl.ds(r0, C)
        inter, scores, vs, new_state = [], [], [], []
        for h in range(H):
            ck = slice(h * dk, (h + 1) * dk)
            b = b_sc[rows, ck]
            q = q_sc[rows, ck]
            k = k_sc[rows, ck]
            v = v_ref[rows, h * dv:(h + 1) * dv].astype(BF16)
            b_last = b[C - 1:C, :]
            qe = (q * jnp.exp(b)).astype(BF16)
            state = st_ref[h]
            inter.append(lax.dot_general(qe, state.astype(BF16), _NT, preferred_element_type=F32))
            if direct:
                def columns(s8, sc, ck=ck, b=b, q=q):
                    r8 = pl.ds(pl.multiple_of(r0 + s8 * SUBLANES, SUBLANES), SUBLANES)
                    b8 = b_sc[r8, ck]
                    k8 = k_sc[r8, ck]
                    for i in range(SUBLANES):
                        decay = jnp.exp(b - b8[i:i + 1, :])
                        contrib = jnp.sum(q * (k8[i:i + 1, :] * decay), axis=-1, keepdims=True)
                        sc = jnp.where(col == s8 * SUBLANES + i, contrib, sc)
                    return sc
                sc = lax.fori_loop(0, C // SUBLANES, columns, jnp.zeros((C, C), F32))
            else:
                ke = (k * jnp.exp(-b)).astype(BF16)
                sc = lax.dot_general(qe, ke, _NT, preferred_element_type=F32)
            scores.append(sc)
            k_tail = (k * jnp.exp(b_last - b)).astype(BF16)
            new_state.append(state * jnp.exp(b_last) + lax.dot_general(
                v, k_tail, _TN, preferred_element_type=F32))
            vs.append(v)
        outs = []
        for h in range(H):
            sc = jnp.where(row >= col, scores[h], 0.0).astype(BF16)
            outs.append(inter[h] + jnp.dot(sc, vs[h], preferred_element_type=F32))
        for h in range(H):
            cv = slice(h * dv, (h + 1) * dv)
            st_ref[h] = new_state[h]
            g = g_ref[rows, cv].astype(F32)
            o_ref[rows, cv] = ((_rms_scale(outs[h]) * gain) * (g * _sigmoid(g))).astype(o_ref.dtype)

    @pl.when(factorable)
    def _():
        lax.fori_loop(0, n_chunks, lambda c, _: (chunk(c, False), 0)[1], 0)

    @pl.when(jnp.logical_not(factorable))
    def _():
        lax.fori_loop(0, n_chunks, lambda c, _: (chunk(c, True), 0)[1], 0)


def _linattn(mode, u, col_blocks, extras, gain, B, S, H, dk, dv):
    C = LINATTN_CHUNK
    Ts = min(LINATTN_TS, S)
    assert S % Ts == 0 and Ts % C == 0
    wk, wv = H * dk, H * dv
    qi, ki, vi, gi = col_blocks
    in_specs = [
        pl.BlockSpec((None, Ts, wk), lambda b, s: (b, s, qi)),
        pl.BlockSpec((None, Ts, wk), lambda b, s: (b, s, ki)),
        pl.BlockSpec((None, Ts, wv), lambda b, s: (b, s, vi)),
        pl.BlockSpec((None, Ts, wv), lambda b, s: (b, s, gi)),
    ]
    args = [u, u, u, u]
    for e in extras:
        if e.ndim == 3:
            in_specs.append(pl.BlockSpec((None, Ts, e.shape[2]), lambda b, s: (b, s, 0)))
        else:
            in_specs.append(pl.BlockSpec(e.shape, lambda b, s: (0, 0)))
        args.append(e)
    in_specs.append(pl.BlockSpec((1, dv), lambda b, s: (0, 0)))
    args.append(gain.reshape(1, dv))
    blocks = (2 * _nbytes((Ts, wk), ACT) + 3 * _nbytes((Ts, wv), ACT) + _nbytes((Ts, LANES), F32)
              + _nbytes((LANES, wk), BF16))
    scratch = 3 * _nbytes((Ts, wk), F32) + _nbytes((H, dv, dk), F32)
    return pl.pallas_call(
        functools.partial(_linattn_kernel, mode=mode, H=H, dk=dk, dv=dv, C=C, Ts=Ts),
        grid=(B, S // Ts),
        in_specs=in_specs,
        out_specs=pl.BlockSpec((None, Ts, wv), lambda b, s: (b, s, 0)),
        out_shape=jax.ShapeDtypeStruct((B, S, wv), ACT),
        scratch_shapes=[
            pltpu.VMEM((H, dv, dk), F32),
            pltpu.VMEM((Ts, wk), F32),
            pltpu.VMEM((Ts, wk), F32),
            pltpu.VMEM((Ts, wk), F32),
        ],
        compiler_params=pltpu.CompilerParams(
            dimension_semantics=("parallel", "arbitrary"),
            vmem_limit_bytes=_vmem_limit(blocks, scratch)),
        name="linattn_" + mode,
    )(*args)


def _fox_gate_kernel(ff_ref, bias_ref, c_ref, *, S, CB):
    r = lax.broadcasted_iota(jnp.int32, (CB, CB), 0)
    c = lax.broadcasted_iota(jnp.int32, (CB, CB), 1)
    tril = jnp.where(r >= c, 1.0, 0.0).astype(F32)
    carry = jnp.zeros((1, LANES), F32)
    for i in range(S // CB):
        rows = slice(i * CB, (i + 1) * CB)
        log_f = _log_sigmoid(ff_ref[rows, :] + bias_ref[...])
        cs = jnp.dot(tril, log_f, precision=lax.Precision.HIGHEST,
                     preferred_element_type=F32) + carry
        c_ref[rows, :] = cs
        carry = cs[CB - 1:CB, :]


def _fox_gate(ff, bias, B, S):
    CB = min(CUMSUM_BLOCK, S)
    assert S % CB == 0
    return pl.pallas_call(
        functools.partial(_fox_gate_kernel, S=S, CB=CB),
        grid=(B,),
        in_specs=[
            pl.BlockSpec((None, S, LANES), lambda b: (b, 0, 0)),
            pl.BlockSpec((1, LANES), lambda b: (0, 0)),
        ],
        out_specs=pl.BlockSpec((None, S, LANES), lambda b: (b, 0, 0)),
        out_shape=jax.ShapeDtypeStruct((B, S, LANES), F32),
        compiler_params=pltpu.CompilerParams(
            dimension_semantics=("parallel",),
            vmem_limit_bytes=_vmem_limit(2 * _nbytes((S, LANES), F32))),
        name="fox_gate",
    )(ff, bias)


def _attn_kernel(*refs, mode, d, BLK, nb, topk, G):
    if mode == "moba":
        q_ref, k_ref, v_ref, slope_ref, o_ref, vt_sc, acc_sc, kmean_sc, relb_sc = refs
    else:
        q_ref, k_ref, v_ref, ccol_ref, crow_ref, o_ref, vt_sc, acc_sc, crep_sc = refs
    assert q_ref.dtype == BF16 and k_ref.dtype == BF16 and v_ref.dtype == BF16
    hg = pl.program_id(1)
    j = pl.program_id(2)
    S = nb * BLK
    key_i = lax.broadcasted_iota(jnp.int32, (BLK, BLK), 0)
    qry_i = lax.broadcasted_iota(jnp.int32, (BLK, BLK), 1)

    @pl.when(j == 0)
    def _():
        for g in range(G):
            cd = slice(g * d, (g + 1) * d)
            if mode == "moba":
                kmean_sc[g] = jnp.zeros(kmean_sc.shape[1:], F32)
                relb_sc[g] = (slope_ref[g, :, :1] * LOG2E) * (qry_i - key_i).astype(F32)
            else:
                lane = lax.broadcasted_iota(jnp.int32, (S, LANES), 1)
                c_key = jnp.sum(jnp.where(lane == hg * G + g, ccol_ref[...], 0.0),
                                axis=-1, keepdims=True)
            for n in range(nb):
                rows = slice(n * BLK, (n + 1) * BLK)
                if mode == "fox":
                    crep_sc[g, n] = jnp.broadcast_to(c_key[rows] * LOG2E, (BLK, LANES))
                vt_sc[g, n] = v_ref[rows, cd].astype(F32).T.astype(BF16)
                if mode == "moba":
                    kmean_sc[g, n:n + 1, :] = jnp.mean(k_ref[rows, cd].astype(F32),
                                                       axis=0, keepdims=True)

    causal = qry_i >= key_i
    qb, slope, selected = [], [], []
    if mode == "moba":
        blk_i = lax.broadcasted_iota(jnp.int32, (kmean_sc.shape[1], BLK), 0)

    for g in range(G):
        q_raw = q_ref[:, g * d:(g + 1) * d]
        qb.append((q_raw.astype(F32) * (d ** -0.5 * LOG2E)).astype(BF16))
        if mode == "moba":
            kmean = kmean_sc[g]
            hi = kmean.astype(BF16).astype(F32)
            mid = (kmean - hi).astype(BF16).astype(F32)
            lo = (kmean - hi) - mid
            nbp = kmean.shape[0]
            parts = lax.dot_general(jnp.concatenate([hi, mid, lo], axis=0).astype(BF16), q_raw,
                                    _NT, preferred_element_type=F32)
            gate = (parts[:nbp] + parts[nbp:2 * nbp] + parts[2 * nbp:]) * (d ** -0.5)
            rank = jnp.zeros(gate.shape, F32)
            for m in range(nb):
                gm = gate[m:m + 1, :]
                beats = jnp.where(gm > gate, 1.0,
                                  jnp.where((gm == gate) & (blk_i > m), 1.0, 0.0))
                rank = rank + beats * (m < j).astype(F32)
            selected.append(jnp.where((blk_i < j) & (rank < topk), 1.0, 0.0))
            slope.append(slope_ref[g, :, :1] * LOG2E)

    def partial(g, n):
        k_blk = k_ref[pl.ds(pl.multiple_of(n * BLK, BLK), BLK), g * d:(g + 1) * d]
        s = lax.dot_general(k_blk, qb[g], _NT, preferred_element_type=F32)
        if mode == "moba":
            return s - relb_sc[g]
        return s - jnp.concatenate([crep_sc[g, n]] * (BLK // LANES), axis=1)

    def adjust(g, n):
        if mode == "moba":
            return -(slope[g] * ((j - n) * BLK).astype(F32))
        return crow_ref[g, pl.ds(j, 1), :] * LOG2E

    scores = [jnp.where(causal, partial(g, j), MASKED_LOGIT) for g in range(G)]
    stats, probs = [], []
    for g in range(G):
        m0 = jnp.max(scores[g], axis=0, keepdims=True)
        p = jnp.exp2(scores[g] - m0)
        stats += [m0 + adjust(g, j), jnp.sum(p, axis=0, keepdims=True)]
        probs.append(p.astype(BF16))
    for g in range(G):
        acc_sc[g] = jnp.dot(vt_sc[g, j], probs[g], preferred_element_type=F32)

    def past_block(n, stats):
        scores = [partial(g, n) for g in range(G)]
        out, probs, alphas = [], [], []
        for g in range(G):
            m_prev, l_prev = stats[2 * g], stats[2 * g + 1]
            s = scores[g]
            if mode == "moba":
                keep = jnp.sum(jnp.where(blk_i == n, selected[g], 0.0), axis=0, keepdims=True)
                s = jnp.where(keep > 0.5, s, MASKED_LOGIT)
            adj = adjust(g, n)
            m_new = jnp.maximum(m_prev, jnp.max(s, axis=0, keepdims=True) + adj)
            alpha = jnp.exp2(m_prev - m_new)
            p = jnp.exp2(s - (m_new - adj))
            out += [m_new, alpha * l_prev + jnp.sum(p, axis=0, keepdims=True)]
            probs.append(p.astype(BF16))
            alphas.append(alpha)
        updates = [jnp.dot(vt_sc[g, n], probs[g], preferred_element_type=F32) for g in range(G)]
        for g in range(G):
            acc_sc[g] = alphas[g] * acc_sc[g] + updates[g]
        return tuple(out)

    stats = lax.fori_loop(0, j, past_block, tuple(stats))
    for g in range(G):
        o_ref[:, g * d:(g + 1) * d] = (acc_sc[g] / stats[2 * g + 1]).T.astype(o_ref.dtype)


def _attention(mode, u, col_blocks, extras, B, S, H, d):
    BLK = min(ATTN_BLOCK, S)
    G = ATTN_HEADS_PER_STEP
    assert S % BLK == 0 and BLK % LANES == 0 and H % G == 0
    nb = S // BLK
    nbp = -(-nb // SUBLANES) * SUBLANES
    qi, ki, vi = (c // G for c in col_blocks)
    assert all(c % G == 0 for c in col_blocks)
    w = G * d
    in_specs = [
        pl.BlockSpec((None, BLK, w), lambda b, h, j: (b, j, qi + h)),
        pl.BlockSpec((None, S, w), lambda b, h, j: (b, 0, ki + h)),
        pl.BlockSpec((None, S, w), lambda b, h, j: (b, 0, vi + h)),
    ]
    scratch = [pltpu.VMEM((G, nb, d, BLK), BF16), pltpu.VMEM((G, d, BLK), F32)]
    scratch_bytes = _nbytes((G, S, d), BF16) + _nbytes((G, d, BLK), F32)
    if mode == "moba":
        (slopes,) = extras
        in_specs.append(pl.BlockSpec((G, 1, LANES), lambda b, h, j: (h, 0, 0)))
        scratch += [pltpu.VMEM((G, nbp, d), F32), pltpu.VMEM((G, BLK, BLK), F32)]
        scratch_bytes += _nbytes((G, BLK, BLK), F32)
        blocks = 2 * _nbytes((S, w), ACT) + 2 * _nbytes((BLK, w), ACT)
    else:
        c_cols, c_rows = extras
        in_specs.append(pl.BlockSpec((None, S, LANES), lambda b, h, j: (b, 0, 0)))
        in_specs.append(pl.BlockSpec((None, G, nb, BLK), lambda b, h, j: (b, h, 0, 0)))
        scratch.append(pltpu.VMEM((G, nb, BLK, LANES), F32))
        scratch_bytes += _nbytes((G, S, LANES), F32)
        blocks = (2 * _nbytes((S, w), ACT) + 2 * _nbytes((BLK, w), ACT)
                  + _nbytes((S, LANES), F32) + _nbytes((G, nb, BLK), F32))
    return pl.pallas_call(
        functools.partial(_attn_kernel, mode=mode, d=d, BLK=BLK, nb=nb, topk=MOBA_TOPK, G=G),
        grid=(B, H // G, nb),
        in_specs=in_specs,
        out_specs=pl.BlockSpec((None, BLK, w), lambda b, h, j: (b, j, h)),
        out_shape=jax.ShapeDtypeStruct((B, S, H * d), ACT),
        scratch_shapes=scratch,
        compiler_params=pltpu.CompilerParams(
            dimension_semantics=("parallel", "parallel", "arbitrary"),
            vmem_limit_bytes=_vmem_limit(blocks, scratch_bytes)),
        name="attn_" + mode,
    )(u, u, u, *extras)


def _split_in_proj(w_in, lo, hi):
    wb = lax.optimization_barrier(w_in.astype(BF16))
    side = jnp.pad(wb[:, lo:hi], ((0, 0), (0, LANES - (hi - lo))))
    return wb, wb[:, hi:], side


def kernel(x, ev_norm_mix, ev_w_in, ev_gla_gate_w2, ev_gla_gate_b, ev_gla_out_norm, ev_w_out, ev_norm_mlp, ev_w_up, ev_w_down, od_norm_mix, od_w_in, od_fox_fgate_b, od_hgrn_out_norm, od_w_out, od_norm_mlp, od_w_up, od_w_down, hgrn_lb_raw, final_norm):
    B, S, D = x.shape
    T = B * S
    depth = hgrn_lb_raw.shape[0]
    lb_soft = jax.nn.softmax(hgrn_lb_raw.astype(F32), axis=0)
    lb_all = jnp.cumsum(lb_soft, axis=0) - lb_soft[0]
    slopes = jnp.exp2(-8.0 * jnp.arange(1, MOBA_HEADS + 1, dtype=F32) / MOBA_HEADS)
    slopes = jnp.broadcast_to(slopes[:, None, None], (MOBA_HEADS, 1, LANES))

    h = x.reshape(T, D)
    for layer in range(depth):
        e = layer // 2
        last = layer == depth - 1
        if layer % 2 == 0:
            gla_w = GLA_HEADS * GLA_DK
            lo = 2 * gla_w + 2 * GLA_HEADS * GLA_DV
            w_all, w_tail, w_side = _split_in_proj(ev_w_in[e], lo, lo + GLA_GATE_RANK)
            u, glr = _norm_matmul(h, ev_norm_mix[e], w_all, lo, w_tail, w_side)
            u = u.reshape(B, S, -1)
            w2 = jnp.pad(ev_gla_gate_w2[e], ((0, LANES - GLA_GATE_RANK), (0, 0))).astype(BF16)
            mix_a = _linattn("gla", u, (0, 1, 1, 2),
                             [glr.reshape(B, S, LANES), w2, ev_gla_gate_b[e].reshape(1, gla_w)],
                             ev_gla_out_norm[e], B, S, GLA_HEADS, GLA_DK, GLA_DV)
            m0 = lo // MOBA_HD
            mix_b = _attention("moba", u, (m0, m0 + MOBA_HEADS, m0 + 2 * MOBA_HEADS), [slopes],
                               B, S, MOBA_HEADS, MOBA_HD)
            w_out, norm_mlp, w_up, w_down = ev_w_out[e], ev_norm_mlp[e], ev_w_up[e], ev_w_down[e]
        else:
            fox_w = FOX_HEADS * FOX_HD
            lo = 3 * fox_w
            w_all, w_tail, w_side = _split_in_proj(od_w_in[e], lo, lo + FOX_HEADS)
            u, ff = _norm_matmul(h, od_norm_mix[e], w_all, lo, w_tail, w_side)
            u = u.reshape(B, S, -1)
            bias = jnp.pad(od_fox_fgate_b[e], (0, LANES - FOX_HEADS)).reshape(1, LANES)
            c_cols = _fox_gate(ff.reshape(B, S, LANES), bias, B, S)
            blk = min(ATTN_BLOCK, S)
            c_rows = c_cols[:, :, :FOX_HEADS].transpose(0, 2, 1).reshape(B, FOX_HEADS, S // blk, blk)
            mix_a = _attention("fox", u, (0, FOX_HEADS, 2 * FOX_HEADS), [c_cols, c_rows],
                               B, S, FOX_HEADS, FOX_HD)
            hw = HGRN_HEADS * HGRN_DK
            h0 = lo // hw
            mix_b = _linattn("hgrn", u, (h0, h0 + 1, h0 + 2, h0 + 3),
                             [lb_all[layer].reshape(1, hw)],
                             od_hgrn_out_norm[e], B, S, HGRN_HEADS, HGRN_DK, HGRN_DV)
            w_out, norm_mlp, w_up, w_down = od_w_out[e], od_norm_mlp[e], od_w_up[e], od_w_down[e]
        h = _out_proj(mix_a.reshape(T, -1), mix_b.reshape(T, -1), w_out.astype(BF16), h)
        h = _mlp(h, norm_mlp, w_up.astype(BF16), w_down.astype(BF16), final_norm, last)
    return h.reshape(B, S, D)
```
